```python
import math
import jax, jax.numpy as jnp
from jax import lax
import numpy as np

D_MODEL = 2048
BATCH = 8
SEQ = 4096
DEPTH = 4
DEC_BATCH = 16
DEC_SEQ = 32
PAST_LEN = 2048

CHUNK = 64
N_HEADS = 8
HEAD_DIM = 128
QK_DIM = 2 * HEAD_DIM
V_DIM = 2 * HEAD_DIM
D_ATTN = N_HEADS * V_DIM
D_CONV = D_MODEL
CONV_WIDTH = 3
D_FF = 4 * D_MODEL
Q_BLOCK = 128
EPS = 1e-6
NEG_INF = -1e30
IN_SIZES = [D_CONV, D_CONV, D_CONV, N_HEADS * QK_DIM, N_HEADS * QK_DIM, D_ATTN, D_MODEL, D_MODEL]
IN_SPLITS = [int(s) for s in np.cumsum(IN_SIZES[:-1])]
D_IN = int(sum(IN_SIZES))

kernel_name = "hybrid_shortconv_diffattn_stream_step"


def _rmsnorm(x, g):
    x32 = x.astype(jnp.float32)
    y = x32 * lax.rsqrt(jnp.mean(x32 * x32, axis=-1, keepdims=True) + EPS)
    return y.astype(x.dtype) * g


def _alibi_slopes():
    return jnp.exp2(-8.0 * (jnp.arange(N_HEADS, dtype=jnp.float32) + 1.0) / N_HEADS)


def _diff_attn_core(q, k, v, q_pos, k_pos, lam):
    s = jnp.einsum('bqhcd,bkhcd->bchqk', q, k, preferred_element_type=jnp.float32)
    dist = jnp.abs(q_pos[:, None] - k_pos[None, :]).astype(jnp.float32)
    bias = -_alibi_slopes()[:, None, None] * dist[None]
    allowed = (k_pos[None, :] // CHUNK) <= (q_pos[:, None] // CHUNK)
    s = jnp.where(allowed[None, None, None], s + bias[None, None], NEG_INF)
    p = jax.nn.softmax(s, axis=-1)
    a = p[:, 0] - lam * p[:, 1]
    return jnp.einsum('bhqk,bkhe->bqhe', a.astype(v.dtype), v)


def _mixer(h, conv_prev, k_past, v_past, lam, lam_init, w_in, conv_w, g_subln, w_conv_out, w_attn_out, w_o):
    Bn, T, _ = h.shape
    proj = h @ w_in
    xin, gc, gb, q, k, v, g_conv, g_attn = jnp.split(proj, IN_SPLITS, axis=-1)
    u = gc * xin
    upad = jnp.concatenate([conv_prev.astype(u.dtype), u], axis=1)
    yc = sum(conv_w[j] * upad[:, j:j + T] for j in range(CONV_WIDTH))
    y_conv = (gb * yc) @ w_conv_out
    new_conv = upad[:, -(CONV_WIDTH - 1):]
    q5 = q.reshape(Bn, T, N_HEADS, 2, HEAD_DIM) * (HEAD_DIM ** -0.5)
    k4 = k.reshape(Bn, T, N_HEADS, QK_DIM)
    v4 = v.reshape(Bn, T, N_HEADS, V_DIM)
    if k_past is None:
        k5 = k4.reshape(Bn, T, N_HEADS, 2, HEAD_DIM)
        k_pos = jnp.arange(T)
        nblk = T // Q_BLOCK
        qb = jnp.moveaxis(q5.reshape(Bn, nblk, Q_BLOCK, N_HEADS, 2, HEAD_DIM), 1, 0)
        posb = k_pos.reshape(nblk, Q_BLOCK)
        ob = lax.map(lambda a: _diff_attn_core(a[0], k5, v4, a[1], k_pos, lam), (qb, posb))
        o = jnp.moveaxis(ob, 0, 1).reshape(Bn, T, N_HEADS, V_DIM)
    else:
        P = k_past.shape[1]
        kf = jnp.concatenate([k_past.astype(k4.dtype), k4], axis=1).reshape(Bn, P + T, N_HEADS, 2, HEAD_DIM)
        vf = jnp.concatenate([v_past.astype(v4.dtype), v4], axis=1)
        o = _diff_attn_core(q5, kf, vf, P + jnp.arange(T), jnp.arange(P + T), lam)
    o = _rmsnorm(o, g_subln) * (1.0 - lam_init)
    y_attn = o.reshape(Bn, T, D_ATTN) @ w_attn_out
    merged = jax.nn.sigmoid(g_conv) * y_conv + jax.nn.sigmoid(g_attn) * y_attn
    return merged @ w_o, k4, v4, new_conv


def _layer(x, c, conv_prev, k_past, v_past, lam, lam_init, w_ada, b_ada, g_mix, w_in, conv_w, g_subln,
           w_conv_out, w_attn_out, w_o, g_mlp, w_up, w_down):
    mod = jax.nn.silu(c) @ w_ada + b_ada
    sh_a, sc_a, ga_a, sh_m, sc_m, ga_m = [m[:, None, :] for m in jnp.split(mod, 6, axis=-1)]
    h = _rmsnorm(x, g_mix) * (1.0 + sc_a) + sh_a
    out, k4, v4, new_conv = _mixer(h, conv_prev, k_past, v_past, lam, lam_init, w_in, conv_w, g_subln,
                                   w_conv_out, w_attn_out, w_o)
    x = x + ga_a * out
    h2 = _rmsnorm(x, g_mlp) * (1.0 + sc_m) + sh_m
    x = x + ga_m * (jnp.square(jax.nn.relu(h2 @ w_up)) @ w_down)
    return x, k4, v4, new_conv


def setup_inputs(seed: int = 0) -> dict:
    key = jax.random.key(seed)
    ks = jax.random.split(key, 24)
    n = lambda k, shape, s: jax.random.normal(k, shape, jnp.float32) * s
    L = DEPTH
    return {
        "x_prompt": n(ks[0], (BATCH, SEQ, D_MODEL), 1.0),
        "x_sample": n(ks[1], (DEC_BATCH, DEC_SEQ, D_MODEL), 1.0),
        "c_prompt": n(ks[2], (BATCH, D_MODEL), 1.0),
        "c_sample": n(ks[3], (DEC_BATCH, D_MODEL), 1.0),
        "cache_k": n(ks[4], (L, DEC_BATCH, PAST_LEN, N_HEADS, QK_DIM), 1.0),
        "cache_v": n(ks[5], (L, DEC_BATCH, PAST_LEN, N_HEADS, V_DIM), 1.0),
        "state_conv": n(ks[6], (L, DEC_BATCH, CONV_WIDTH - 1, D_CONV), 1.0),
        "w_ada": n(ks[7], (L, D_MODEL, 6 * D_MODEL), 0.5 * D_MODEL ** -0.5),
        "b_ada": n(ks[8], (L, 6 * D_MODEL), 0.02),
        "g_mix": 1.0 + n(ks[9], (L, D_MODEL), 0.05),
        "w_in": n(ks[10], (L, D_MODEL, D_IN), D_MODEL ** -0.5),
        "conv_w": n(ks[11], (L, CONV_WIDTH, D_CONV), 0.5),
        "lambda_q1": n(ks[12], (L, HEAD_DIM), 0.1),
        "lambda_k1": n(ks[13], (L, HEAD_DIM), 0.1),
        "lambda_q2": n(ks[14], (L, HEAD_DIM), 0.1),
        "lambda_k2": n(ks[15], (L, HEAD_DIM), 0.1),
        "g_subln": 1.0 + n(ks[16], (L, V_DIM), 0.05),
        "w_conv_out": n(ks[17], (L, D_CONV, D_MODEL), D_CONV ** -0.5),
        "w_attn_out": n(ks[18], (L, D_ATTN, D_MODEL), D_ATTN ** -0.5),
        "w_o": n(ks[19], (L, D_MODEL, D_MODEL), D_MODEL ** -0.5),
        "g_mlp": 1.0 + n(ks[20], (L, D_MODEL), 0.05),
        "w_up": n(ks[21], (L, D_MODEL, D_FF), D_MODEL ** -0.5),
        "w_down": n(ks[22], (L, D_FF, D_MODEL), D_FF ** -0.5),
        "g_final": 1.0 + n(ks[23], (D_MODEL,), 0.05),
    }


def reference(x_prompt, x_sample, c_prompt, c_sample, cache_k, cache_v, state_conv, w_ada, b_ada, g_mix,
              w_in, conv_w, lambda_q1, lambda_k1, lambda_q2, lambda_k2, g_subln, w_conv_out, w_attn_out,
              w_o, g_mlp, w_up, w_down, g_final):
    xp, xs = x_prompt, x_sample
    kp_l, vp_l, cp_l, ks_l, vs_l, cs_l = [], [], [], [], [], []
    conv_zero = jnp.zeros((x_prompt.shape[0], CONV_WIDTH - 1, D_CONV), x_prompt.dtype)
    for l in range(DEPTH):
        lam_init = 0.8 - 0.6 * math.exp(-0.3 * l)
        lam = (jnp.exp(jnp.sum(lambda_q1[l].astype(jnp.float32) * lambda_k1[l].astype(jnp.float32)))
               - jnp.exp(jnp.sum(lambda_q2[l].astype(jnp.float32) * lambda_k2[l].astype(jnp.float32)))
               + lam_init)
        w = (w_ada[l], b_ada[l], g_mix[l], w_in[l], conv_w[l], g_subln[l], w_conv_out[l], w_attn_out[l],
             w_o[l], g_mlp[l], w_up[l], w_down[l])
        xp, kp, vp, cp = _layer(xp, c_prompt, conv_zero, None, None, lam, lam_init, *w)
        xs, ksm, vsm, csm = _layer(xs, c_sample, state_conv[l], cache_k[l], cache_v[l], lam, lam_init, *w)
        kp_l.append(kp); vp_l.append(vp); cp_l.append(cp)
        ks_l.append(ksm); vs_l.append(vsm); cs_l.append(csm)
    y_prompt = _rmsnorm(xp, g_final)
    y_sample = _rmsnorm(xs, g_final)
    k_prompt = jnp.stack(kp_l); v_prompt = jnp.stack(vp_l); conv_prompt = jnp.stack(cp_l)
    k_sample = jnp.stack(ks_l); v_sample = jnp.stack(vs_l); conv_sample = jnp.stack(cs_l)
    return (y_prompt, y_sample, k_prompt, v_prompt, conv_prompt, k_sample, v_sample, conv_sample)
```

```python
import functools
import math

import jax
import jax.numpy as jnp
from jax import lax
from jax.experimental import pallas as pl
from jax.experimental.pallas import tpu as pltpu

CHUNK = 64
CONV_WIDTH = 3
EPS = 1e-6
NEG_INF = -1e30
HEAD_DIM = 128
HEAD_WIDTH = 2 * HEAD_DIM

V7X_VMEM_BYTES = 64 * 1024 * 1024
VMEM_HEADROOM_BYTES = 8 * 1024 * 1024

BF16 = jnp.bfloat16
F32 = jnp.float32


def _vmem_limit(*block_bytes, scratch=0):
    need = 2 * sum(block_bytes) + scratch + VMEM_HEADROOM_BYTES
    return int(min(need, V7X_VMEM_BYTES - 4 * 1024 * 1024))


def _nbytes(shape, dtype):
    return math.prod(shape) * jnp.dtype(dtype).itemsize


def _mod_spec(per_row, which, tm, tn, tiles_per_batch, col_from_j):
    if per_row:
        return pl.BlockSpec((None, None, tm, tn),
                            lambda i, j: (0, which, i, j if col_from_j else 0))
    return pl.BlockSpec((None, None, 1, tn),
                        lambda i, j: (i // tiles_per_batch, which, 0, j if col_from_j else 0))


def _normmod(x, g, scale, shift):
    ms = jnp.mean(x * x, axis=-1, keepdims=True)
    return (x * lax.rsqrt(ms + EPS)) * g * (1.0 + scale) + shift


def _mod_kernel(c_ref, w_ref, b_ref, o_ref):
    c = c_ref[...]
    a = (c * jax.nn.sigmoid(c)).astype(BF16)
    y = jnp.dot(a, w_ref[...].astype(BF16), preferred_element_type=F32)
    o_ref[...] = y + b_ref[...]


def _modulation(c_all, w_ada, b_ada):
    n_layers, d, d6 = w_ada.shape
    rows = c_all.shape[0]
    tn = 1024
    return pl.pallas_call(
        _mod_kernel,
        out_shape=jax.ShapeDtypeStruct((n_layers, rows, d6), F32),
        grid=(n_layers, d6 // tn),
        in_specs=[
            pl.BlockSpec((rows, d), lambda l, j: (0, 0)),
            pl.BlockSpec((None, d, tn), lambda l, j: (l, 0, j)),
            pl.BlockSpec((None, 1, tn), lambda l, j: (l, 0, j)),
        ],
        out_specs=pl.BlockSpec((None, rows, tn), lambda l, j: (l, 0, j)),
        compiler_params=pltpu.CompilerParams(
            dimension_semantics=("arbitrary", "arbitrary"),
            vmem_limit_bytes=_vmem_limit(_nbytes((d, tn), F32), _nbytes((rows, d), F32),
                                         scratch=_nbytes((d, tn), BF16))),
        name="modulation",
    )(c_all, w_ada, b_ada.reshape(n_layers, 1, d6))


def _normproj_kernel(x_ref, g_ref, sc_ref, sh_ref, w_ref, o_ref, h_ref, *, out_scale):
    @pl.when(pl.program_id(1) == 0)
    def _():
        h = _normmod(x_ref[...], g_ref[...], sc_ref[...], sh_ref[...])
        h_ref[...] = h.astype(h_ref.dtype)

    y = jnp.dot(h_ref[...], w_ref[...], preferred_element_type=F32)
    if out_scale != 1.0:
        y = y * out_scale
    o_ref[...] = y.astype(o_ref.dtype)


def _normproj(x, g, mod, w, *, which_scale, which_shift, col_start, n_cols, out_dtype,
              per_row, tokens_per_batch, out_scale=1.0, tm, tn=1024, name):
    n, d = x.shape
    tm = min(tm, n)
    tiles_per_batch = max(tokens_per_batch // tm, 1)
    col_blk0 = col_start // tn
    kern = functools.partial(_normproj_kernel, out_scale=out_scale)
    return pl.pallas_call(
        kern,
        out_shape=jax.ShapeDtypeStruct((n, n_cols), out_dtype),
        grid=(n // tm, n_cols // tn),
        in_specs=[
            pl.BlockSpec((tm, d), lambda i, j: (i, 0)),
            pl.BlockSpec((1, d), lambda i, j: (0, 0)),
            _mod_spec(per_row, which_scale, tm, d, tiles_per_batch, False),
            _mod_spec(per_row, which_shift, tm, d, tiles_per_batch, False),
            pl.BlockSpec((d, tn), lambda i, j: (0, col_blk0 + j)),
        ],
        out_specs=pl.BlockSpec((tm, tn), lambda i, j: (i, j)),
        scratch_shapes=[pltpu.VMEM((tm, d), BF16)],
        compiler_params=pltpu.CompilerParams(
            dimension_semantics=("parallel", "arbitrary"),
            vmem_limit_bytes=_vmem_limit(
                _nbytes((tm, d), F32), _nbytes((d, tn), BF16), _nbytes((tm, tn), out_dtype),
                2 * _nbytes((tm if per_row else 8, d), F32),
                scratch=_nbytes((tm, d), BF16) + _nbytes((tm, d), F32))),
        name=name,
    )(x, g.reshape(1, d), mod, mod, w)


_CONV_PAD = 8


def _conv_kernel(xin_ref, gc_ref, gb_ref, prev_ref, w_ref, z_ref, state_ref, upad_ref, *, tt):
    t = pl.program_id(1)
    halo = CONV_WIDTH - 1

    @pl.when(t == 0)
    def _():
        upad_ref[_CONV_PAD - halo:_CONV_PAD, :] = prev_ref[...]

    u = gc_ref[...].astype(F32) * xin_ref[...].astype(F32)
    upad_ref[_CONV_PAD:_CONV_PAD + tt, :] = u
    w = w_ref[...]
    yc = w[0:1, :] * upad_ref[_CONV_PAD - 2:_CONV_PAD - 2 + tt, :]
    yc = yc + w[1:2, :] * upad_ref[_CONV_PAD - 1:_CONV_PAD - 1 + tt, :]
    yc = yc + w[2:3, :] * u
    z_ref[...] = (gb_ref[...].astype(F32) * yc).astype(z_ref.dtype)
    tail = upad_ref[_CONV_PAD + tt - halo:_CONV_PAD + tt, :]
    state_ref[...] = tail
    upad_ref[_CONV_PAD - halo:_CONV_PAD, :] = tail


def _gated_conv(xgb, conv_prev, conv_w, *, batch, seq, tt):
    n = xgb.shape[0]
    d = xgb.shape[1] // 3
    tt = min(tt, seq)
    nt = seq // tt
    halo = CONV_WIDTH - 1
    kern = functools.partial(_conv_kernel, tt=tt)
    row = lambda b, t: b * nt + t
    return pl.pallas_call(
        kern,
        out_shape=(jax.ShapeDtypeStruct((n, d), BF16),
                   jax.ShapeDtypeStruct((batch, halo, d), F32)),
        grid=(batch, nt),
        in_specs=[
            pl.BlockSpec((tt, d), lambda b, t: (row(b, t), 0)),
            pl.BlockSpec((tt, d), lambda b, t: (row(b, t), 1)),
            pl.BlockSpec((tt, d), lambda b, t: (row(b, t), 2)),
            pl.BlockSpec((None, halo, d), lambda b, t: (b, 0, 0)),
            pl.BlockSpec((CONV_WIDTH, d), lambda b, t: (0, 0)),
        ],
        out_specs=(pl.BlockSpec((tt, d), lambda b, t: (row(b, t), 0)),
                   pl.BlockSpec((None, halo, d), lambda b, t: (b, 0, 0))),
        scratch_shapes=[pltpu.VMEM((_CONV_PAD + tt, d), F32)],
        compiler_params=pltpu.CompilerParams(
            dimension_semantics=("parallel", "arbitrary"),
            vmem_limit_bytes=_vmem_limit(4 * _nbytes((tt, d), BF16),
                                         scratch=3 * _nbytes((_CONV_PAD + tt, d), F32))),
        name="gated_conv",
    )(xgb, xgb, xgb, conv_prev, conv_w)


def _lambda_value(lam_ref):
    e1 = jnp.exp(jnp.sum(lam_ref[0:1, :] * lam_ref[1:2, :], axis=-1, keepdims=True))
    e2 = jnp.exp(jnp.sum(lam_ref[2:3, :] * lam_ref[3:4, :], axis=-1, keepdims=True))
    lam_init = lam_ref[4:5, 0:1]
    return e1 - e2 + lam_init, lam_init


def _online_softmax_update(s, v, m_ref, l_ref, acc_ref, idx):
    m_prev = m_ref[idx]
    m_new = jnp.maximum(m_prev, jnp.max(s, axis=-1, keepdims=True))
    alpha = jnp.exp(m_prev - m_new)
    p = jnp.exp(s - m_new)
    l_ref[idx] = alpha * l_ref[idx] + jnp.sum(p, axis=-1, keepdims=True)
    acc_ref[idx] = alpha * acc_ref[idx] + jnp.dot(p.astype(BF16), v, preferred_element_type=F32)
    m_ref[idx] = m_new


def _diff_head_output(acc_ref, l_ref, i1, i2, lam, lam_init, g_subln):
    o = acc_ref[i1] / l_ref[i1] - lam * (acc_ref[i2] / l_ref[i2])
    ms = jnp.mean(o * o, axis=-1, keepdims=True)
    return (o * lax.rsqrt(ms + EPS)) * g_subln * (1.0 - lam_init)


def _qk_scores(q, k):
    return lax.dot_general(q, k, (((1,), (1,)), ((), ())), preferred_element_type=F32)


def _attn_kernel(q_ref, k_ref, v_ref, lam_ref, slope_ref, gs_ref, o_ref,
                 m_ref, l_ref, acc_ref, *, tq, tk):
    qi = pl.program_id(2)
    ki = pl.program_id(3)

    @pl.when(ki == 0)
    def _():
        m_ref[...] = jnp.full(m_ref.shape, NEG_INF, F32)
        l_ref[...] = jnp.zeros(l_ref.shape, F32)
        acc_ref[...] = jnp.zeros(acc_ref.shape, F32)

    def step(masked):
        q = q_ref[...]
        k = k_ref[...].astype(BF16)
        v = v_ref[...].astype(BF16)
        row = lax.broadcasted_iota(jnp.int32, (tq, tk), 0) + qi * tq
        col = lax.broadcasted_iota(jnp.int32, (tq, tk), 1) + ki * tk
        slope = slope_ref[0:1, 0:1]
        bias = -slope * jnp.abs(row - col).astype(F32)
        if masked:
            allowed = (col // CHUNK) <= (row // CHUNK)
        for c in range(2):
            lanes = slice(c * HEAD_DIM, (c + 1) * HEAD_DIM)
            s = _qk_scores(q[:, lanes], k[:, lanes]) + bias
            if masked:
                s = jnp.where(allowed, s, NEG_INF)
            _online_softmax_update(s, v, m_ref, l_ref, acc_ref, c)

    @pl.when(ki < qi)
    def _():
        step(False)

    @pl.when(ki == qi)
    def _():
        step(True)
        lam, lam_init = _lambda_value(lam_ref)
        o = _diff_head_output(acc_ref, l_ref, 0, 1, lam, lam_init, gs_ref[...])
        o_ref[...] = o.astype(o_ref.dtype)


def _attention_prompt(q, k, v, lam_rows, slopes, g_subln, *, batch, seq, n_heads, tile):
    n, d = q.shape
    tq = tk = min(tile, seq)
    nq = seq // tq
    kern = functools.partial(_attn_kernel, tq=tq, tk=tk)
    kv_spec = pl.BlockSpec((tk, HEAD_WIDTH),
                           lambda b, h, i, j: (b * nq + jnp.minimum(j, i), h))
    return pl.pallas_call(
        kern,
        out_shape=jax.ShapeDtypeStruct((n, d), BF16),
        grid=(batch, n_heads, nq, nq),
        in_specs=[
            pl.BlockSpec((tq, HEAD_WIDTH), lambda b, h, i, j: (b * nq + i, h)),
            kv_spec,
            kv_spec,
            pl.BlockSpec((8, HEAD_DIM), lambda b, h, i, j: (0, 0)),
            pl.BlockSpec((None, 1, HEAD_DIM), lambda b, h, i, j: (h, 0, 0)),
            pl.BlockSpec((1, HEAD_WIDTH), lambda b, h, i, j: (0, 0)),
        ],
        out_specs=pl.BlockSpec((tq, HEAD_WIDTH), lambda b, h, i, j: (b * nq + i, h)),
        scratch_shapes=[
            pltpu.VMEM((2, tq, 1), F32),
            pltpu.VMEM((2, tq, 1), F32),
            pltpu.VMEM((2, tq, HEAD_WIDTH), F32),
        ],
        compiler_params=pltpu.CompilerParams(
            dimension_semantics=("parallel", "parallel", "parallel", "arbitrary"),
            vmem_limit_bytes=_vmem_limit(
                2 * _nbytes((tk, HEAD_WIDTH), F32), 2 * _nbytes((tq, HEAD_WIDTH), BF16),
                scratch=4 * _nbytes((tq, 128), F32) + 2 * _nbytes((tq, HEAD_WIDTH), F32)
                + 8 * _nbytes((tq, tk), F32))),
        name="diff_attention_prompt",
    )(q, k, v, lam_rows, slopes, g_subln.reshape(1, HEAD_WIDTH))


def _attn_decode_kernel(q_ref, kp_ref, vp_ref, kn_ref, vn_ref, lam_ref, gs_ref, o_ref,
                        m_ref, l_ref, acc_ref, *, tk, past, seq, n_heads):
    ki = pl.program_id(1)

    @pl.when(ki == 0)
    def _():
        m_ref[...] = jnp.full(m_ref.shape, NEG_INF, F32)
        l_ref[...] = jnp.zeros(l_ref.shape, F32)
        acc_ref[...] = jnp.zeros(acc_ref.shape, F32)

    def process(k_ref, v_ref, kpos0, n_keys):
        qpos = lax.broadcasted_iota(jnp.int32, (seq, n_keys), 0) + past
        kpos = lax.broadcasted_iota(jnp.int32, (seq, n_keys), 1) + kpos0
        dist = jnp.abs(qpos - kpos).astype(F32)
        allowed = (kpos // CHUNK) <= (qpos // CHUNK)
        for h in range(n_heads):
            slope = 2.0 ** (-8.0 * (h + 1) / n_heads)
            bias = -slope * dist
            v = v_ref[:, h * HEAD_WIDTH:(h + 1) * HEAD_WIDTH].astype(BF16)
            for c in range(2):
                lanes = slice(h * HEAD_WIDTH + c * HEAD_DIM, h * HEAD_WIDTH + (c + 1) * HEAD_DIM)
                s = _qk_scores(q_ref[:, lanes], k_ref[:, lanes].astype(BF16)) + bias
                s = jnp.where(allowed, s, NEG_INF)
                _online_softmax_update(s, v, m_ref, l_ref, acc_ref, 2 * h + c)

    process(kp_ref, vp_ref, ki * tk, tk)

    @pl.when(ki == pl.num_programs(1) - 1)
    def _():
        process(kn_ref, vn_ref, past, seq)
        lam, lam_init = _lambda_value(lam_ref)
        for h in range(n_heads):
            o = _diff_head_output(acc_ref, l_ref, 2 * h, 2 * h + 1, lam, lam_init, gs_ref[...])
            o_ref[:, h * HEAD_WIDTH:(h + 1) * HEAD_WIDTH] = o.astype(o_ref.dtype)


def _attention_decode(q, k_new, v_new, k_past, v_past, layer, lam_rows, g_subln, *, batch, seq,
                      n_heads, tk):
    n, d = q.shape
    past = k_past.shape[1]
    tk = min(tk, past)
    kern = functools.partial(_attn_decode_kernel, tk=tk, past=past, seq=seq, n_heads=n_heads)
    past_spec = pl.BlockSpec((None, tk, d), lambda b, j: (layer * batch + b, j, 0))
    new_spec = pl.BlockSpec((seq, d), lambda b, j: (b, 0))
    return pl.pallas_call(
        kern,
        out_shape=jax.ShapeDtypeStruct((n, d), BF16),
        grid=(batch, past // tk),
        in_specs=[
            new_spec, past_spec, past_spec, new_spec, new_spec,
            pl.BlockSpec((8, HEAD_DIM), lambda b, j: (0, 0)),
            pl.BlockSpec((1, HEAD_WIDTH), lambda b, j: (0, 0)),
        ],
        out_specs=new_spec,
        scratch_shapes=[
            pltpu.VMEM((2 * n_heads, seq, 1), F32),
            pltpu.VMEM((2 * n_heads, seq, 1), F32),
            pltpu.VMEM((2 * n_heads, seq, HEAD_WIDTH), F32),
        ],
        compiler_params=pltpu.CompilerParams(
            dimension_semantics=("parallel", "arbitrary"),
            vmem_limit_bytes=_vmem_limit(
                2 * _nbytes((tk, d), F32), 4 * _nbytes((seq, d), F32),
                scratch=4 * n_heads * _nbytes((seq, 128), F32)
                + 2 * n_heads * _nbytes((seq, HEAD_WIDTH), F32) + _nbytes((tk, d), F32))),
        name="diff_attention_decode",
    )(q, k_past, v_past, k_new, v_new, lam_rows, g_subln.reshape(1, HEAD_WIDTH))


def _merge_kernel(z_ref, o_ref, gc_ref, ga_ref, wc_ref, wa_ref, out_ref):
    y_conv = jnp.dot(z_ref[...], wc_ref[...], preferred_element_type=F32)
    y_attn = jnp.dot(o_ref[...], wa_ref[...], preferred_element_type=F32)
    merged = (jax.nn.sigmoid(gc_ref[...].astype(F32)) * y_conv
              + jax.nn.sigmoid(ga_ref[...].astype(F32)) * y_attn)
    out_ref[...] = merged.astype(out_ref.dtype)


def _merge_branches(z, o, gates, w_conv_out, w_attn_out, *, tm, tn=512):
    n, d = z.shape
    tm = min(tm, n)
    gate_blocks = d // tn
    return pl.pallas_call(
        _merge_kernel,
        out_shape=jax.ShapeDtypeStruct((n, d), BF16),
        grid=(n // tm, d // tn),
        in_specs=[
            pl.BlockSpec((tm, d), lambda i, j: (i, 0)),
            pl.BlockSpec((tm, d), lambda i, j: (i, 0)),
            pl.BlockSpec((tm, tn), lambda i, j: (i, j)),
            pl.BlockSpec((tm, tn), lambda i, j: (i, gate_blocks + j)),
            pl.BlockSpec((d, tn), lambda i, j: (0, j)),
            pl.BlockSpec((d, tn), lambda i, j: (0, j)),
        ],
        out_specs=pl.BlockSpec((tm, tn), lambda i, j: (i, j)),
        compiler_params=pltpu.CompilerParams(
            dimension_semantics=("parallel", "arbitrary"),
            vmem_limit_bytes=_vmem_limit(
                2 * _nbytes((tm, d), BF16), 2 * _nbytes((d, tn), BF16),
                3 * _nbytes((tm, tn), BF16), scratch=4 * _nbytes((tm, tn), F32))),
        name="merge_branches",
    )(z, o, gates, gates, w_conv_out, w_attn_out)


def _proj_residual_kernel(a_ref, w_ref, x_ref, gate_ref, o_ref):
    y = jnp.dot(a_ref[...], w_ref[...], preferred_element_type=F32)
    o_ref[...] = x_ref[...] + gate_ref[...] * y


def _proj_residual(a, w, x, mod, *, which_gate, per_row, tokens_per_batch, tm, tn=1024):
    n, d = x.shape
    tm = min(tm, n)
    tiles_per_batch = max(tokens_per_batch // tm, 1)
    return pl.pallas_call(
        _proj_residual_kernel,
        out_shape=jax.ShapeDtypeStruct((n, d), F32),
        grid=(n // tm, d // tn),
        in_specs=[
            pl.BlockSpec((tm, a.shape[1]), lambda i, j: (i, 0)),
            pl.BlockSpec((a.shape[1], tn), lambda i, j: (0, j)),
            pl.BlockSpec((tm, tn), lambda i, j: (i, j)),
            _mod_spec(per_row, which_gate, tm, tn, tiles_per_batch, True),
        ],
        out_specs=pl.BlockSpec((tm, tn), lambda i, j: (i, j)),
        compiler_params=pltpu.CompilerParams(
            dimension_semantics=("parallel", "arbitrary"),
            vmem_limit_bytes=_vmem_limit(
                _nbytes((tm, a.shape[1]), BF16), _nbytes((a.shape[1], tn), BF16),
                3 * _nbytes((tm, tn), F32), scratch=_nbytes((tm, tn), F32))),
        name="proj_residual",
    )(a, w, x, mod)


def _mlp_kernel(x_ref, g_ref, sc_ref, sh_ref, gate_ref, wup_ref, wdn_ref, o_ref, h_ref, acc_ref):
    kf = pl.program_id(1)

    @pl.when(kf == 0)
    def _():
        h = _normmod(x_ref[...], g_ref[...], sc_ref[...], sh_ref[...])
        h_ref[...] = h.astype(h_ref.dtype)
        acc_ref[...] = jnp.zeros(acc_ref.shape, F32)

    up = jnp.dot(h_ref[...], wup_ref[...], preferred_element_type=F32)
    act = jnp.square(jnp.maximum(up, 0.0)).astype(BF16)
    acc_ref[...] += jnp.dot(act, wdn_ref[...], preferred_element_type=F32)

    @pl.when(kf == pl.num_programs(1) - 1)
    def _():
        o_ref[...] = x_ref[...] + gate_ref[...] * acc_ref[...]


def _mlp(x, g, mod, w_up, w_down, *, per_row, tokens_per_batch, tm, tf=1024):
    n, d = x.shape
    d_ff = w_up.shape[1]
    tm = min(tm, n)
    tiles_per_batch = max(tokens_per_batch // tm, 1)
    return pl.pallas_call(
        _mlp_kernel,
        out_shape=jax.ShapeDtypeStruct((n, d), F32),
        grid=(n // tm, d_ff // tf),
        in_specs=[
            pl.BlockSpec((tm, d), lambda i, j: (i, 0)),
            pl.BlockSpec((1, d), lambda i, j: (0, 0)),
            _mod_spec(per_row, 4, tm, d, tiles_per_batch, False),
            _mod_spec(per_row, 3, tm, d, tiles_per_batch, False),
            _mod_spec(per_row, 5, tm, d, tiles_per_batch, False),
            pl.BlockSpec((d, tf), lambda i, j: (0, j)),
            pl.BlockSpec((tf, d), lambda i, j: (j, 0)),
        ],
        out_specs=pl.BlockSpec((tm, d), lambda i, j: (i, 0)),
        scratch_shapes=[pltpu.VMEM((tm, d), BF16), pltpu.VMEM((tm, d), F32)],
        compiler_params=pltpu.CompilerParams(
            dimension_semantics=("parallel", "arbitrary"),
            vmem_limit_bytes=_vmem_limit(
                2 * _nbytes((tm, d), F32), 2 * _nbytes((d, tf), BF16),
                3 * _nbytes((tm if per_row else 8, d), F32),
                scratch=_nbytes((tm, d), BF16) + _nbytes((tm, d), F32)
                + _nbytes((tm, tf), F32))),
        name="mlp",
    )(x, g.reshape(1, d), mod, mod, mod, w_up, w_down)


def _final_norm_kernel(x_ref, g_ref, o_ref):
    x = x_ref[...]
    ms = jnp.mean(x * x, axis=-1, keepdims=True)
    o_ref[...] = (x * lax.rsqrt(ms + EPS)) * g_ref[...]


def _final_norm(x, g, *, tm):
    n, d = x.shape
    tm = min(tm, n)
    return pl.pallas_call(
        _final_norm_kernel,
        out_shape=jax.ShapeDtypeStruct((n, d), F32),
        grid=(n // tm,),
        in_specs=[pl.BlockSpec((tm, d), lambda i: (i, 0)), pl.BlockSpec((1, d), lambda i: (0, 0))],
        out_specs=pl.BlockSpec((tm, d), lambda i: (i, 0)),
        compiler_params=pltpu.CompilerParams(
            dimension_semantics=("parallel",),
            vmem_limit_bytes=_vmem_limit(2 * _nbytes((tm, d), F32))),
        name="final_norm",
    )(x, g.reshape(1, d))


def _layer(x, mod, conv_prev, kv_past, lam_rows, slopes, w, *, batch, seq, per_row, n_heads, tm):
    g_mix, w_in, conv_w, g_subln, w_conv_out, w_attn_out, w_o, g_mlp, w_up, w_down = w
    n, d = x.shape
    proj = functools.partial(_normproj, x, g_mix, mod, w_in, which_scale=1, which_shift=0,
                             per_row=per_row, tokens_per_batch=seq, tm=tm)
    xgb = proj(col_start=0, n_cols=3 * d, out_dtype=BF16, name="in_proj_conv")
    q = proj(col_start=3 * d, n_cols=d, out_dtype=BF16, out_scale=HEAD_DIM ** -0.5,
             name="in_proj_q")
    k = proj(col_start=4 * d, n_cols=d, out_dtype=F32, name="in_proj_k")
    v = proj(col_start=5 * d, n_cols=d, out_dtype=F32, name="in_proj_v")
    gates = proj(col_start=6 * d, n_cols=2 * d, out_dtype=BF16, name="in_proj_gates")

    z, new_conv = _gated_conv(xgb, conv_prev, conv_w, batch=batch, seq=seq, tt=512)

    if kv_past is None:
        o = _attention_prompt(q, k, v, lam_rows, slopes, g_subln, batch=batch, seq=seq,
                              n_heads=n_heads, tile=512)
    else:
        o = _attention_decode(q, k, v, *kv_past, lam_rows, g_subln,
                              batch=batch, seq=seq, n_heads=n_heads, tk=512)

    merged = _merge_branches(z, o, gates, w_conv_out, w_attn_out, tm=tm)
    x = _proj_residual(merged, w_o, x, mod, which_gate=2, per_row=per_row,
                       tokens_per_batch=seq, tm=tm)
    x = _mlp(x, g_mlp, mod, w_up, w_down, per_row=per_row, tokens_per_batch=seq, tm=512)
    return x, k, v, new_conv


def kernel(x_prompt, x_sample, c_prompt, c_sample, cache_k, cache_v, state_conv, w_ada, b_ada, g_mix, w_in, conv_w, lambda_q1, lambda_k1, lambda_q2, lambda_k2, g_subln, w_conv_out, w_attn_out, w_o, g_mlp, w_up, w_down, g_final):
    bp, tp, d = x_prompt.shape
    bs, ts, _ = x_sample.shape
    n_layers = w_in.shape[0]
    n_heads = cache_k.shape[3]
    halo = CONV_WIDTH - 1

    mod_all = _modulation(jnp.concatenate([c_prompt, c_sample], axis=0), w_ada, b_ada)
    slopes = jnp.broadcast_to(
        jnp.exp2(-8.0 * (jnp.arange(n_heads, dtype=F32) + 1.0) / n_heads)[:, None, None],
        (n_heads, 1, HEAD_DIM))

    xp = x_prompt.reshape(bp * tp, d)
    xs = x_sample.reshape(bs * ts, d)
    conv_zero = jnp.zeros((bp, halo, d), F32)
    past_len = cache_k.shape[2]
    past_k = cache_k.reshape(n_layers * bs, past_len, d)
    past_v = cache_v.reshape(n_layers * bs, past_len, d)
    outs = [[] for _ in range(6)]
    for l in range(n_layers):
        lam_init = 0.8 - 0.6 * math.exp(-0.3 * l)
        lam_rows = jnp.concatenate(
            [lambda_q1[l][None], lambda_k1[l][None], lambda_q2[l][None], lambda_k2[l][None],
             jnp.full((4, HEAD_DIM), lam_init, F32)], axis=0).astype(F32)
        w = (g_mix[l], w_in[l].astype(BF16), conv_w[l], g_subln[l], w_conv_out[l].astype(BF16),
             w_attn_out[l].astype(BF16), w_o[l].astype(BF16), g_mlp[l], w_up[l].astype(BF16),
             w_down[l].astype(BF16))
        mod_p = mod_all[l, :bp].reshape(bp, 6, 1, d)
        mod_s = jnp.repeat(mod_all[l, bp:].reshape(bs, 6, d), ts, axis=0)
        mod_s = jnp.transpose(mod_s, (1, 0, 2))[None]
        xp, kp, vp, cp = _layer(xp, mod_p, conv_zero, None, lam_rows, slopes, w, batch=bp,
                                seq=tp, per_row=False, n_heads=n_heads, tm=1024)
        xs, ks, vs, cs = _layer(xs, mod_s, state_conv[l], (past_k, past_v, l), lam_rows,
                                slopes, w, batch=bs, seq=ts, per_row=True, n_heads=n_heads,
                                tm=512)
        for lst, val in zip(outs, (kp, vp, cp, ks, vs, cs)):
            lst.append(val)

    y_prompt = _final_norm(xp, g_final, tm=512).reshape(bp, tp, d)
    y_sample = _final_norm(xs, g_final, tm=512).reshape(bs, ts, d)
    kv_shape_p = (n_layers, bp, tp, n_heads, HEAD_WIDTH)
    kv_shape_s = (n_layers, bs, ts, n_heads, HEAD_WIDTH)
    return (y_prompt, y_sample,
            jnp.stack(outs[0]).reshape(kv_shape_p), jnp.stack(outs[1]).reshape(kv_shape_p),
            jnp.stack(outs[2]),
            jnp.stack(outs[3]).reshape(kv_shape_s), jnp.stack(outs[4]).reshape(kv_shape_s),
            jnp.stack(outs[5]))
```

```python
import functools
import math

import jax
import jax.numpy as jnp
from jax import lax
from jax.experimental import pallas as pl
from jax.experimental.pallas import tpu as pltpu

CHUNK = 64
CONV_WIDTH = 3
CONV_HALO = CONV_WIDTH - 1
EPS = 1e-6
NEG_INF = -1e30
HEAD_DIM = 128
HEAD_WIDTH = 2 * HEAD_DIM
LOG2E = math.log2(math.e)
LANES = 128

V7X_VMEM_BYTES = 64 * 1024 * 1024
VMEM_HEADROOM_BYTES = 8 * 1024 * 1024

BF16 = jnp.bfloat16
F32 = jnp.float32


def _vmem_limit(*block_bytes, scratch=0):
    need = 2 * sum(block_bytes) + scratch + VMEM_HEADROOM_BYTES
    return int(min(need, V7X_VMEM_BYTES - 4 * 1024 * 1024))


def _nbytes(shape, dtype):
    return math.prod(shape) * jnp.dtype(dtype).itemsize


def _mod_spec(per_row, which, tm, tn, tiles_per_batch, col_from_j):
    if per_row:
        return pl.BlockSpec((None, None, tm, tn),
                            lambda i, j: (0, which, i, j if col_from_j else 0))
    return pl.BlockSpec((None, None, 1, tn),
                        lambda i, j: (i // tiles_per_batch, which, 0, j if col_from_j else 0))


def _normmod(x, g, scale, shift):
    ms = jnp.mean(x * x, axis=-1, keepdims=True)
    return (x * lax.rsqrt(ms + EPS)) * g * (1.0 + scale) + shift


def _mod_kernel(c_ref, w_ref, b_ref, o_ref):
    c = c_ref[...]
    a = (c * jax.nn.sigmoid(c)).astype(BF16)
    y = jnp.dot(a, w_ref[...].astype(BF16), preferred_element_type=F32)
    o_ref[...] = y + b_ref[...]


def _modulation(c_all, w_ada, b_ada):
    n_layers, d, d6 = w_ada.shape
    rows = c_all.shape[0]
    tn = 1024
    return pl.pallas_call(
        _mod_kernel,
        out_shape=jax.ShapeDtypeStruct((n_layers, rows, d6), F32),
        grid=(n_layers, d6 // tn),
        in_specs=[
            pl.BlockSpec((rows, d), lambda l, j: (0, 0)),
            pl.BlockSpec((None, d, tn), lambda l, j: (l, 0, j)),
            pl.BlockSpec((None, 1, tn), lambda l, j: (l, 0, j)),
        ],
        out_specs=pl.BlockSpec((None, rows, tn), lambda l, j: (l, 0, j)),
        compiler_params=pltpu.CompilerParams(
            dimension_semantics=("arbitrary", "arbitrary"),
            vmem_limit_bytes=_vmem_limit(_nbytes((d, tn), F32), _nbytes((rows, d), F32),
                                         scratch=_nbytes((d, tn), BF16))),
        name="modulation",
    )(c_all, w_ada, b_ada.reshape(n_layers, 1, d6))


def _normproj_kernel(x_ref, g_ref, sc_ref, sh_ref, w_ref, o_ref, h_ref):
    @pl.when(pl.program_id(1) == 0)
    def _():
        h = _normmod(x_ref[...], g_ref[...], sc_ref[...], sh_ref[...])
        h_ref[...] = h.astype(h_ref.dtype)

    y = jnp.dot(h_ref[...], w_ref[...], preferred_element_type=F32)
    o_ref[...] = y.astype(o_ref.dtype)


def _normproj(x, g, mod, w, *, n_cols, per_row, tokens_per_batch, tm, tn=1024):
    n, d = x.shape
    tm = min(tm, n)
    tiles_per_batch = max(tokens_per_batch // tm, 1)
    return pl.pallas_call(
        _normproj_kernel,
        out_shape=jax.ShapeDtypeStruct((n, n_cols), BF16),
        grid=(n // tm, n_cols // tn),
        in_specs=[
            pl.BlockSpec((tm, d), lambda i, j: (i, 0)),
            pl.BlockSpec((1, d), lambda i, j: (0, 0)),
            _mod_spec(per_row, 1, tm, d, tiles_per_batch, False),
            _mod_spec(per_row, 0, tm, d, tiles_per_batch, False),
            pl.BlockSpec((d, tn), lambda i, j: (0, j)),
        ],
        out_specs=pl.BlockSpec((tm, tn), lambda i, j: (i, j)),
        scratch_shapes=[pltpu.VMEM((tm, d), BF16)],
        compiler_params=pltpu.CompilerParams(
            dimension_semantics=("parallel", "arbitrary"),
            vmem_limit_bytes=_vmem_limit(
                _nbytes((tm, d), F32), _nbytes((d, tn), BF16), _nbytes((tm, tn), BF16),
                2 * _nbytes((tm if per_row else 8, d), F32),
                scratch=_nbytes((tm, d), BF16) + _nbytes((tm, d), F32))),
        name="in_proj_conv_inputs",
    )(x, g.reshape(1, d), mod, mod, w)


_CONV_PAD = 8


def _causal_conv(u, gb, w, upad_ref, rows):
    upad_ref[_CONV_PAD:_CONV_PAD + rows, :] = u
    yc = w[0:1, :] * upad_ref[_CONV_PAD - 2:_CONV_PAD - 2 + rows, :]
    yc = yc + w[1:2, :] * upad_ref[_CONV_PAD - 1:_CONV_PAD - 1 + rows, :]
    yc = yc + w[2:3, :] * u
    return gb * yc


def _conv_kernel(xin_ref, gc_ref, gb_ref, prev_ref, w_ref, z_ref, state_ref, upad_ref, *, tt):
    t = pl.program_id(1)

    @pl.when(t == 0)
    def _():
        upad_ref[_CONV_PAD - CONV_HALO:_CONV_PAD, :] = prev_ref[...]

    u = gc_ref[...].astype(F32) * xin_ref[...].astype(F32)
    z = _causal_conv(u, gb_ref[...].astype(F32), w_ref[...], upad_ref, tt)
    z_ref[...] = z.astype(z_ref.dtype)
    tail = upad_ref[_CONV_PAD + tt - CONV_HALO:_CONV_PAD + tt, :]
    state_ref[...] = tail
    upad_ref[_CONV_PAD - CONV_HALO:_CONV_PAD, :] = tail


def _gated_conv(xgb, conv_prev, conv_w, *, batch, seq, tt):
    n = xgb.shape[0]
    d = xgb.shape[1] // 3
    tt = min(tt, seq)
    nt = seq // tt
    kern = functools.partial(_conv_kernel, tt=tt)
    row = lambda b, t: b * nt + t
    return pl.pallas_call(
        kern,
        out_shape=(jax.ShapeDtypeStruct((n, d), BF16),
                   jax.ShapeDtypeStruct((batch, CONV_HALO, d), F32)),
        grid=(batch, nt),
        in_specs=[
            pl.BlockSpec((tt, d), lambda b, t: (row(b, t), 0)),
            pl.BlockSpec((tt, d), lambda b, t: (row(b, t), 1)),
            pl.BlockSpec((tt, d), lambda b, t: (row(b, t), 2)),
            pl.BlockSpec((None, CONV_HALO, d), lambda b, t: (b, 0, 0)),
            pl.BlockSpec((CONV_WIDTH, d), lambda b, t: (0, 0)),
        ],
        out_specs=(pl.BlockSpec((tt, d), lambda b, t: (row(b, t), 0)),
                   pl.BlockSpec((None, CONV_HALO, d), lambda b, t: (b, 0, 0))),
        scratch_shapes=[pltpu.VMEM((_CONV_PAD + tt, d), F32)],
        compiler_params=pltpu.CompilerParams(
            dimension_semantics=("parallel", "arbitrary"),
            vmem_limit_bytes=_vmem_limit(4 * _nbytes((tt, d), BF16),
                                         scratch=3 * _nbytes((_CONV_PAD + tt, d), F32))),
        name="gated_conv",
    )(xgb, xgb, xgb, conv_prev, conv_w)


def _inproj_conv_kernel(x_ref, g_ref, sc_ref, sh_ref, wx_ref, wc_ref, wb_ref, prev_ref, cw_ref,
                        z_ref, state_ref, h_ref, upad_ref, carry_ref, *, tm, tiles_per_batch):
    i = pl.program_id(0)
    j = pl.program_id(1)

    @pl.when(j == 0)
    def _():
        h = _normmod(x_ref[...], g_ref[...], sc_ref[...], sh_ref[...])
        h_ref[...] = h.astype(h_ref.dtype)

    first = (i % tiles_per_batch) == 0

    @pl.when(first)
    def _():
        upad_ref[_CONV_PAD - CONV_HALO:_CONV_PAD, :] = prev_ref[...]

    @pl.when(jnp.logical_not(first))
    def _():
        upad_ref[_CONV_PAD - CONV_HALO:_CONV_PAD, :] = carry_ref[j]

    h = h_ref[...]
    xin = jnp.dot(h, wx_ref[...], preferred_element_type=F32)
    gc = jnp.dot(h, wc_ref[...], preferred_element_type=F32)
    gb = jnp.dot(h, wb_ref[...], preferred_element_type=F32)
    z = _causal_conv(gc * xin, gb, cw_ref[...], upad_ref, tm)
    z_ref[...] = z.astype(z_ref.dtype)
    tail = upad_ref[_CONV_PAD + tm - CONV_HALO:_CONV_PAD + tm, :]
    state_ref[...] = tail
    carry_ref[j] = tail


def _inproj_conv(x, g, mod, w_in, conv_prev, conv_w, *, tokens_per_batch, tm, tn=512):
    n, d = x.shape
    assert tokens_per_batch % tm == 0
    tiles_per_batch = tokens_per_batch // tm
    nb = d // tn
    batch = n // tokens_per_batch
    kern = functools.partial(_inproj_conv_kernel, tm=tm, tiles_per_batch=tiles_per_batch)
    w_spec = lambda seg: pl.BlockSpec((d, tn), lambda i, j: (0, seg * nb + j))
    z, tails = pl.pallas_call(
        kern,
        out_shape=(jax.ShapeDtypeStruct((n, d), BF16),
                   jax.ShapeDtypeStruct((n // tm, CONV_HALO, d), F32)),
        grid=(n // tm, nb),
        in_specs=[
            pl.BlockSpec((tm, d), lambda i, j: (i, 0)),
            pl.BlockSpec((1, d), lambda i, j: (0, 0)),
            _mod_spec(False, 1, tm, d, tiles_per_batch, False),
            _mod_spec(False, 0, tm, d, tiles_per_batch, False),
            w_spec(0), w_spec(1), w_spec(2),
            pl.BlockSpec((None, CONV_HALO, tn), lambda i, j: (i // tiles_per_batch, 0, j)),
            pl.BlockSpec((CONV_WIDTH, tn), lambda i, j: (0, j)),
        ],
        out_specs=(pl.BlockSpec((tm, tn), lambda i, j: (i, j)),
                   pl.BlockSpec((None, CONV_HALO, tn), lambda i, j: (i, 0, j))),
        scratch_shapes=[pltpu.VMEM((tm, d), BF16),
                        pltpu.VMEM((_CONV_PAD + tm, tn), F32),
                        pltpu.VMEM((nb, CONV_HALO, tn), F32)],
        compiler_params=pltpu.CompilerParams(
            dimension_semantics=("arbitrary", "arbitrary"),
            vmem_limit_bytes=_vmem_limit(
                _nbytes((tm, d), F32), 3 * _nbytes((d, tn), BF16), _nbytes((tm, tn), BF16),
                scratch=_nbytes((tm, d), BF16) + 6 * _nbytes((tm, tn), F32))),
        name="in_proj_conv",
    )(x, g.reshape(1, d), mod, mod, w_in, w_in, w_in, conv_prev, conv_w)
    return z, tails.reshape(batch, tiles_per_batch, CONV_HALO, d)[:, -1]


def _inproj_qkvg_kernel(*refs, nb, q_scale, n_carried):
    (x_ref, g_ref, sc_ref, sh_ref, w_ref) = refs[:5]
    (q_ref, k32_ref, kbf_ref, v32_ref, vbf_ref, gates_ref, h_ref) = refs[5 + n_carried:]
    j = pl.program_id(1)
    heads_per_step = w_ref.shape[1] // HEAD_WIDTH

    @pl.when(j == 0)
    def _():
        h = _normmod(x_ref[...], g_ref[...], sc_ref[...], sh_ref[...])
        h_ref[...] = h.astype(h_ref.dtype)

    y = jnp.dot(h_ref[...], w_ref[...], preferred_element_type=F32)

    def store_heads(head_major_ref, step):
        for hh in range(heads_per_step):
            head_major_ref[:, step * heads_per_step + hh, :] = (
                y[:, hh * HEAD_WIDTH:(hh + 1) * HEAD_WIDTH])

    @pl.when(j < nb)
    def _():
        q_ref[...] = (y * q_scale).astype(q_ref.dtype)

    for step in range(nb):
        @pl.when(j == nb + step)
        def _():
            kbf_ref[...] = y.astype(kbf_ref.dtype)
            store_heads(k32_ref, step)

        @pl.when(j == 2 * nb + step)
        def _():
            vbf_ref[...] = y.astype(vbf_ref.dtype)
            store_heads(v32_ref, step)

    @pl.when(j >= 3 * nb)
    def _():
        gates_ref[...] = y.astype(gates_ref.dtype)


def _inproj_qkvg(x, g, mod, w_in, kv_all, *, layer, n_layers, n_heads, per_row, tokens_per_batch,
                 q_scale, tm, tn=512):
    n, d = x.shape
    tm = min(tm, n)
    tiles_per_batch = max(tokens_per_batch // tm, 1)
    nb = d // tn
    carried = () if kv_all is None else tuple(kv_all)
    kern = functools.partial(_inproj_qkvg_kernel, nb=nb, q_scale=q_scale, n_carried=len(carried))
    seg = lambda first, blocks: pl.BlockSpec(
        (tm, tn), lambda i, j: (i, jnp.clip(j - first, 0, blocks - 1)))
    out = lambda cols: jax.ShapeDtypeStruct((n, cols), BF16)
    head_major = jax.ShapeDtypeStruct((n_layers, n, n_heads, HEAD_WIDTH), F32)
    head_major_spec = pl.BlockSpec((None, tm, n_heads, HEAD_WIDTH), lambda i, j: (layer, i, 0, 0))
    n_in = 5
    q, k_all, k_bf, v_all, v_bf, gates = pl.pallas_call(
        kern,
        out_shape=(out(d), head_major, out(d), head_major, out(d), out(2 * d)),
        grid=(n // tm, 5 * nb),
        in_specs=[
            pl.BlockSpec((tm, d), lambda i, j: (i, 0)),
            pl.BlockSpec((1, d), lambda i, j: (0, 0)),
            _mod_spec(per_row, 1, tm, d, tiles_per_batch, False),
            _mod_spec(per_row, 0, tm, d, tiles_per_batch, False),
            pl.BlockSpec((d, tn), lambda i, j: (0, 3 * nb + j)),
        ] + [pl.BlockSpec(memory_space=pl.ANY)] * len(carried),
        out_specs=(seg(0, nb), head_major_spec, seg(nb, nb), head_major_spec, seg(2 * nb, nb),
                   seg(3 * nb, 2 * nb)),
        input_output_aliases={n_in: 1, n_in + 1: 3} if carried else {},
        scratch_shapes=[pltpu.VMEM((tm, d), BF16)],
        compiler_params=pltpu.CompilerParams(
            dimension_semantics=("arbitrary", "arbitrary"),
            vmem_limit_bytes=_vmem_limit(
                _nbytes((tm, d), F32), _nbytes((d, tn), BF16), 4 * _nbytes((tm, tn), BF16),
                2 * _nbytes((tm, d), F32), 2 * _nbytes((tm if per_row else 8, d), F32),
                scratch=_nbytes((tm, d), BF16) + 2 * _nbytes((tm, tn), F32))),
        name="in_proj_qkvg",
    )(x, g.reshape(1, d), mod, mod, w_in, *carried)
    return q, k_bf, v_bf, gates, (k_all, v_all)


def _lambda_value(lam_ref):
    e1 = jnp.exp(jnp.sum(lam_ref[0:1, :] * lam_ref[1:2, :], axis=-1, keepdims=True))
    e2 = jnp.exp(jnp.sum(lam_ref[2:3, :] * lam_ref[3:4, :], axis=-1, keepdims=True))
    lam_init = lam_ref[4:5, 0:1]
    return e1 - e2 + lam_init, lam_init


def _widen(stat, width):
    if width < LANES:
        return stat[:, :width]
    return pltpu.repeat(stat, width // LANES, axis=1)


def _softmax_block_update(s, v, m_ref, l_ref, acc_ref, idx):
    m_prev = m_ref[idx]
    m_new = jnp.maximum(m_prev, jnp.max(s, axis=-1, keepdims=True))
    alpha = jnp.exp2(m_prev - m_new)
    p = jnp.exp2(s - _widen(m_new, s.shape[1]))
    l_ref[idx] = alpha * l_ref[idx] + jnp.sum(p, axis=-1, keepdims=True)
    pv = jnp.dot(p.astype(BF16), v, preferred_element_type=F32)
    acc_ref[idx] = _widen(alpha, HEAD_WIDTH) * acc_ref[idx] + pv
    m_ref[idx] = m_new


def _diff_head_output(acc_ref, l_ref, i1, i2, lam, lam_init, g_subln):
    o = (acc_ref[i1] / _widen(l_ref[i1], HEAD_WIDTH)
         - lam * (acc_ref[i2] / _widen(l_ref[i2], HEAD_WIDTH)))
    ms = jnp.mean(o * o, axis=-1, keepdims=True)
    return (o * lax.rsqrt(ms + EPS)) * g_subln * (1.0 - lam_init)


def _qk_scores(q, k):
    return lax.dot_general(q, k, (((1,), (1,)), ((), ())), preferred_element_type=F32)


def _init_softmax_state(m_ref, l_ref, acc_ref):
    m_ref[...] = jnp.full(m_ref.shape, NEG_INF, F32)
    l_ref[...] = jnp.zeros(l_ref.shape, F32)
    acc_ref[...] = jnp.zeros(acc_ref.shape, F32)


def _attn_kernel(q_ref, k_ref, v_ref, lam_ref, slope_ref, gs_ref, o_ref,
                 m_ref, l_ref, acc_ref, dbias_ref, *, tile):
    qi = pl.program_id(2)
    slope = slope_ref[0:1, 0:1]

    @pl.when(qi == 0)
    def _():
        t = lax.broadcasted_iota(jnp.int32, (tile, tile), 0)
        s = lax.broadcasted_iota(jnp.int32, (tile, tile), 1)
        bias = slope * (t - jnp.abs(t - s)).astype(F32)
        dbias_ref[...] = jnp.where((s // CHUNK) <= (t // CHUNK), bias, NEG_INF)

    _init_softmax_state(m_ref, l_ref, acc_ref)
    q = q_ref[...]

    def visit(ki, add_bias):
        k0 = pl.multiple_of(ki * tile, tile)
        k = k_ref[pl.ds(k0, tile), :]
        v = v_ref[pl.ds(k0, tile), :]
        for c in range(2):
            lanes = slice(c * HEAD_DIM, (c + 1) * HEAD_DIM)
            s = add_bias(_qk_scores(q[:, lanes], k[:, lanes]))
            _softmax_block_update(s, v, m_ref, l_ref, acc_ref, c)

    def earlier_tile(ki):
        rel = lax.broadcasted_iota(jnp.int32, (1, tile), 1) + (ki - qi) * tile
        col_bias = slope * rel.astype(F32)
        visit(ki, lambda s: s + col_bias)

    def earlier_pair(kp, carry):
        earlier_tile(2 * kp)
        earlier_tile(2 * kp + 1)
        return carry

    lax.fori_loop(0, qi // 2, earlier_pair, 0)

    @pl.when(qi % 2 == 1)
    def _():
        earlier_tile(qi - 1)

    visit(qi, lambda s: s + dbias_ref[...])

    lam, lam_init = _lambda_value(lam_ref)
    o = _diff_head_output(acc_ref, l_ref, 0, 1, lam, lam_init, gs_ref[...])
    o_ref[...] = o.astype(o_ref.dtype)


def _attention_prompt(q, k, v, lam_rows, slopes, g_subln, *, batch, seq, n_heads, tile):
    n, d = q.shape
    tile = min(tile, seq)
    assert seq % tile == 0 and tile % CHUNK == 0
    nq = seq // tile
    kern = functools.partial(_attn_kernel, tile=tile)
    kv_spec = pl.BlockSpec((seq, HEAD_WIDTH), lambda b, h, i: (b, h))
    return pl.pallas_call(
        kern,
        out_shape=jax.ShapeDtypeStruct((n, d), BF16),
        grid=(batch, n_heads, nq),
        in_specs=[
            pl.BlockSpec((tile, HEAD_WIDTH), lambda b, h, i: (b * nq + i, h)),
            kv_spec,
            kv_spec,
            pl.BlockSpec((8, HEAD_DIM), lambda b, h, i: (0, 0)),
            pl.BlockSpec((None, 1, HEAD_DIM), lambda b, h, i: (h, 0, 0)),
            pl.BlockSpec((1, HEAD_WIDTH), lambda b, h, i: (0, 0)),
        ],
        out_specs=pl.BlockSpec((tile, HEAD_WIDTH), lambda b, h, i: (b * nq + i, h)),
        scratch_shapes=[
            pltpu.VMEM((2, tile, LANES), F32),
            pltpu.VMEM((2, tile, LANES), F32),
            pltpu.VMEM((2, tile, HEAD_WIDTH), F32),
            pltpu.VMEM((tile, tile), F32),
        ],
        compiler_params=pltpu.CompilerParams(
            dimension_semantics=("parallel", "parallel", "arbitrary"),
            vmem_limit_bytes=_vmem_limit(
                2 * _nbytes((seq, HEAD_WIDTH), BF16), 2 * _nbytes((tile, HEAD_WIDTH), BF16),
                scratch=4 * _nbytes((tile, LANES), F32) + 2 * _nbytes((tile, HEAD_WIDTH), F32)
                + 9 * _nbytes((tile, tile), F32))),
        name="diff_attention_prompt",
    )(q, k, v, lam_rows, slopes, g_subln.reshape(1, HEAD_WIDTH))


def _attn_decode_kernel(q_ref, kp_ref, vp_ref, kn_ref, vn_ref, lam_ref, gs_ref, o_ref,
                        m_ref, l_ref, acc_ref, *, tk, past, seq, n_heads):
    ki = pl.program_id(1)

    @pl.when(ki == 0)
    def _():
        _init_softmax_state(m_ref, l_ref, acc_ref)

    def process(k_ref, v_ref, kpos0, n_keys):
        qpos = lax.broadcasted_iota(jnp.int32, (seq, n_keys), 0) + past
        kpos = lax.broadcasted_iota(jnp.int32, (seq, n_keys), 1) + kpos0
        dist = jnp.abs(qpos - kpos).astype(F32)
        allowed = (kpos // CHUNK) <= (qpos // CHUNK)
        for h in range(n_heads):
            slope = LOG2E * 2.0 ** (-8.0 * (h + 1) / n_heads)
            bias = -slope * dist
            v = v_ref[:, h * HEAD_WIDTH:(h + 1) * HEAD_WIDTH].astype(BF16)
            for c in range(2):
                lanes = slice(h * HEAD_WIDTH + c * HEAD_DIM, h * HEAD_WIDTH + (c + 1) * HEAD_DIM)
                s = _qk_scores(q_ref[:, lanes], k_ref[:, lanes].astype(BF16)) + bias
                s = jnp.where(allowed, s, NEG_INF)
                _softmax_block_update(s, v, m_ref, l_ref, acc_ref, 2 * h + c)

    process(kp_ref, vp_ref, ki * tk, tk)

    @pl.when(ki == pl.num_programs(1) - 1)
    def _():
        process(kn_ref, vn_ref, past, seq)
        lam, lam_init = _lambda_value(lam_ref)
        for h in range(n_heads):
            o = _diff_head_output(acc_ref, l_ref, 2 * h, 2 * h + 1, lam, lam_init, gs_ref[...])
            o_ref[:, h * HEAD_WIDTH:(h + 1) * HEAD_WIDTH] = o.astype(o_ref.dtype)


def _attention_decode(q, k_new, v_new, k_past, v_past, layer, lam_rows, g_subln, *, batch, seq,
                      n_heads, tk):
    n, d = q.shape
    past = k_past.shape[1]
    tk = min(tk, past)
    kern = functools.partial(_attn_decode_kernel, tk=tk, past=past, seq=seq, n_heads=n_heads)
    past_spec = pl.BlockSpec((None, tk, d), lambda b, j: (layer * batch + b, j, 0))
    new_spec = pl.BlockSpec((seq, d), lambda b, j: (b, 0))
    return pl.pallas_call(
        kern,
        out_shape=jax.ShapeDtypeStruct((n, d), BF16),
        grid=(batch, past // tk),
        in_specs=[
            new_spec, past_spec, past_spec, new_spec, new_spec,
            pl.BlockSpec((8, HEAD_DIM), lambda b, j: (0, 0)),
            pl.BlockSpec((1, HEAD_WIDTH), lambda b, j: (0, 0)),
        ],
        out_specs=new_spec,
        scratch_shapes=[
            pltpu.VMEM((2 * n_heads, seq, LANES), F32),
            pltpu.VMEM((2 * n_heads, seq, LANES), F32),
            pltpu.VMEM((2 * n_heads, seq, HEAD_WIDTH), F32),
        ],
        compiler_params=pltpu.CompilerParams(
            dimension_semantics=("parallel", "arbitrary"),
            vmem_limit_bytes=_vmem_limit(
                2 * _nbytes((tk, d), F32), 4 * _nbytes((seq, d), F32),
                scratch=4 * n_heads * _nbytes((seq, LANES), F32)
                + 2 * n_heads * _nbytes((seq, HEAD_WIDTH), F32) + _nbytes((tk, d), F32))),
        name="diff_attention_decode",
    )(q, k_past, v_past, k_new, v_new, lam_rows, g_subln.reshape(1, HEAD_WIDTH))


def _merge_kernel(z_ref, o_ref, gc_ref, ga_ref, wc_ref, wa_ref, out_ref):
    y_conv = jnp.dot(z_ref[...], wc_ref[...], preferred_element_type=F32)
    y_attn = jnp.dot(o_ref[...], wa_ref[...], preferred_element_type=F32)
    merged = (jax.nn.sigmoid(gc_ref[...].astype(F32)) * y_conv
              + jax.nn.sigmoid(ga_ref[...].astype(F32)) * y_attn)
    out_ref[...] = merged.astype(out_ref.dtype)


def _merge_branches(z, o, gates, w_conv_out, w_attn_out, *, tm, tn=512):
    n, d = z.shape
    tm = min(tm, n)
    gate_blocks = d // tn
    return pl.pallas_call(
        _merge_kernel,
        out_shape=jax.ShapeDtypeStruct((n, d), BF16),
        grid=(n // tm, d // tn),
        in_specs=[
            pl.BlockSpec((tm, d), lambda i, j: (i, 0)),
            pl.BlockSpec((tm, d), lambda i, j: (i, 0)),
            pl.BlockSpec((tm, tn), lambda i, j: (i, j)),
            pl.BlockSpec((tm, tn), lambda i, j: (i, gate_blocks + j)),
            pl.BlockSpec((d, tn), lambda i, j: (0, j)),
            pl.BlockSpec((d, tn), lambda i, j: (0, j)),
        ],
        out_specs=pl.BlockSpec((tm, tn), lambda i, j: (i, j)),
        compiler_params=pltpu.CompilerParams(
            dimension_semantics=("parallel", "arbitrary"),
            vmem_limit_bytes=_vmem_limit(
                2 * _nbytes((tm, d), BF16), 2 * _nbytes((d, tn), BF16),
                3 * _nbytes((tm, tn), BF16), scratch=4 * _nbytes((tm, tn), F32))),
        name="merge_branches",
    )(z, o, gates, gates, w_conv_out, w_attn_out)


def _proj_residual_kernel(a_ref, w_ref, x_ref, gate_ref, o_ref):
    y = jnp.dot(a_ref[...], w_ref[...], preferred_element_type=F32)
    o_ref[...] = x_ref[...] + gate_ref[...] * y


def _proj_residual(a, w, x, mod, *, which_gate, per_row, tokens_per_batch, tm, tn=1024):
    n, d = x.shape
    tm = min(tm, n)
    tiles_per_batch = max(tokens_per_batch // tm, 1)
    return pl.pallas_call(
        _proj_residual_kernel,
        out_shape=jax.ShapeDtypeStruct((n, d), F32),
        grid=(n // tm, d // tn),
        in_specs=[
            pl.BlockSpec((tm, a.shape[1]), lambda i, j: (i, 0)),
            pl.BlockSpec((a.shape[1], tn), lambda i, j: (0, j)),
            pl.BlockSpec((tm, tn), lambda i, j: (i, j)),
            _mod_spec(per_row, which_gate, tm, tn, tiles_per_batch, True),
        ],
        out_specs=pl.BlockSpec((tm, tn), lambda i, j: (i, j)),
        compiler_params=pltpu.CompilerParams(
            dimension_semantics=("parallel", "arbitrary"),
            vmem_limit_bytes=_vmem_limit(
                _nbytes((tm, a.shape[1]), BF16), _nbytes((a.shape[1], tn), BF16),
                3 * _nbytes((tm, tn), F32), scratch=_nbytes((tm, tn), F32))),
        name="proj_residual",
    )(a, w, x, mod)


def _mlp_kernel(x_ref, g_ref, sc_ref, sh_ref, gate_ref, wup_ref, wdn_ref, o_ref, h_ref, acc_ref):
    kf = pl.program_id(1)

    @pl.when(kf == 0)
    def _():
        h = _normmod(x_ref[...], g_ref[...], sc_ref[...], sh_ref[...])
        h_ref[...] = h.astype(h_ref.dtype)
        acc_ref[...] = jnp.zeros(acc_ref.shape, F32)

    up = jnp.dot(h_ref[...], wup_ref[...], preferred_element_type=F32)
    act = jnp.square(jnp.maximum(up, 0.0)).astype(BF16)
    acc_ref[...] += jnp.dot(act, wdn_ref[...], preferred_element_type=F32)

    @pl.when(kf == pl.num_programs(1) - 1)
    def _():
        o_ref[...] = x_ref[...] + gate_ref[...] * acc_ref[...]


def _mlp(x, g, mod, w_up, w_down, *, per_row, tokens_per_batch, tm, tf=1024):
    n, d = x.shape
    d_ff = w_up.shape[1]
    tm = min(tm, n)
    tiles_per_batch = max(tokens_per_batch // tm, 1)
    return pl.pallas_call(
        _mlp_kernel,
        out_shape=jax.ShapeDtypeStruct((n, d), F32),
        grid=(n // tm, d_ff // tf),
        in_specs=[
            pl.BlockSpec((tm, d), lambda i, j: (i, 0)),
            pl.BlockSpec((1, d), lambda i, j: (0, 0)),
            _mod_spec(per_row, 4, tm, d, tiles_per_batch, False),
            _mod_spec(per_row, 3, tm, d, tiles_per_batch, False),
            _mod_spec(per_row, 5, tm, d, tiles_per_batch, False),
            pl.BlockSpec((d, tf), lambda i, j: (0, j)),
            pl.BlockSpec((tf, d), lambda i, j: (j, 0)),
        ],
        out_specs=pl.BlockSpec((tm, d), lambda i, j: (i, 0)),
        scratch_shapes=[pltpu.VMEM((tm, d), BF16), pltpu.VMEM((tm, d), F32)],
        compiler_params=pltpu.CompilerParams(
            dimension_semantics=("parallel", "arbitrary"),
            vmem_limit_bytes=_vmem_limit(
                2 * _nbytes((tm, d), F32), 2 * _nbytes((d, tf), BF16),
                3 * _nbytes((tm if per_row else 8, d), F32),
                scratch=_nbytes((tm, d), BF16) + _nbytes((tm, d), F32)
                + _nbytes((tm, tf), F32))),
        name="mlp",
    )(x, g.reshape(1, d), mod, mod, mod, w_up, w_down)


def _final_norm_kernel(x_ref, g_ref, o_ref):
    x = x_ref[...]
    ms = jnp.mean(x * x, axis=-1, keepdims=True)
    o_ref[...] = (x * lax.rsqrt(ms + EPS)) * g_ref[...]


def _final_norm(x, g, *, tm):
    n, d = x.shape
    tm = min(tm, n)
    return pl.pallas_call(
        _final_norm_kernel,
        out_shape=jax.ShapeDtypeStruct((n, d), F32),
        grid=(n // tm,),
        in_specs=[pl.BlockSpec((tm, d), lambda i: (i, 0)), pl.BlockSpec((1, d), lambda i: (0, 0))],
        out_specs=pl.BlockSpec((tm, d), lambda i: (i, 0)),
        compiler_params=pltpu.CompilerParams(
            dimension_semantics=("parallel",),
            vmem_limit_bytes=_vmem_limit(2 * _nbytes((tm, d), F32))),
        name="final_norm",
    )(x, g.reshape(1, d))


def _layer(x, mod, conv_prev, kv_past, kv_all, lam_rows, slopes, w, *, layer, n_layers, batch, seq,
           per_row, n_heads, tm):
    g_mix, w_in, conv_w, g_subln, w_conv_out, w_attn_out, w_o, g_mlp, w_up, w_down = w
    n, d = x.shape
    if seq % tm == 0 and not per_row:
        z, new_conv = _inproj_conv(x, g_mix, mod, w_in, conv_prev, conv_w,
                                   tokens_per_batch=seq, tm=tm)
    else:
        xgb = _normproj(x, g_mix, mod, w_in, n_cols=3 * d, per_row=per_row,
                        tokens_per_batch=seq, tm=tm)
        z, new_conv = _gated_conv(xgb, conv_prev, conv_w, batch=batch, seq=seq, tt=512)
    q, k_bf, v_bf, gates, kv_all = _inproj_qkvg(
        x, g_mix, mod, w_in, kv_all, layer=layer, n_layers=n_layers, n_heads=n_heads,
        per_row=per_row, tokens_per_batch=seq, q_scale=LOG2E * HEAD_DIM ** -0.5, tm=512)

    if kv_past is None:
        o = _attention_prompt(q, k_bf, v_bf, lam_rows, slopes, g_subln, batch=batch, seq=seq,
                              n_heads=n_heads, tile=512)
    else:
        o = _attention_decode(q, k_bf, v_bf, *kv_past, lam_rows, g_subln,
                              batch=batch, seq=seq, n_heads=n_heads, tk=512)

    merged = _merge_branches(z, o, gates, w_conv_out, w_attn_out, tm=tm)
    x = _proj_residual(merged, w_o, x, mod, which_gate=2, per_row=per_row,
                       tokens_per_batch=seq, tm=tm)
    x = _mlp(x, g_mlp, mod, w_up, w_down, per_row=per_row, tokens_per_batch=seq, tm=512)
    return x, kv_all, new_conv


def kernel(x_prompt, x_sample, c_prompt, c_sample, cache_k, cache_v, state_conv, w_ada, b_ada, g_mix, w_in, conv_w, lambda_q1, lambda_k1, lambda_q2, lambda_k2, g_subln, w_conv_out, w_attn_out, w_o, g_mlp, w_up, w_down, g_final):
    bp, tp, d = x_prompt.shape
    bs, ts, _ = x_sample.shape
    n_layers = w_in.shape[0]
    n_heads = cache_k.shape[3]

    mod_all = _modulation(jnp.concatenate([c_prompt, c_sample], axis=0), w_ada, b_ada)
    slopes = jnp.broadcast_to(
        LOG2E * jnp.exp2(-8.0 * (jnp.arange(n_heads, dtype=F32) + 1.0) / n_heads)[:, None, None],
        (n_heads, 1, HEAD_DIM))

    xp = x_prompt.reshape(bp * tp, d)
    xs = x_sample.reshape(bs * ts, d)
    conv_zero = jnp.zeros((bp, CONV_HALO, d), F32)
    past_len = cache_k.shape[2]
    past_k = cache_k.reshape(n_layers * bs, past_len, d)
    past_v = cache_v.reshape(n_layers * bs, past_len, d)
    conv_p, conv_s = [], []
    kv_p = kv_s = None
    for l in range(n_layers):
        lam_init = 0.8 - 0.6 * math.exp(-0.3 * l)
        lam_rows = jnp.concatenate(
            [lambda_q1[l][None], lambda_k1[l][None], lambda_q2[l][None], lambda_k2[l][None],
             jnp.full((4, HEAD_DIM), lam_init, F32)], axis=0).astype(F32)
        w = (g_mix[l], w_in[l].astype(BF16), conv_w[l], g_subln[l], w_conv_out[l].astype(BF16),
             w_attn_out[l].astype(BF16), w_o[l].astype(BF16), g_mlp[l], w_up[l].astype(BF16),
             w_down[l].astype(BF16))
        mod_p = mod_all[l, :bp].reshape(bp, 6, 1, d)
        mod_s = jnp.repeat(mod_all[l, bp:].reshape(bs, 6, d), ts, axis=0)
        mod_s = jnp.transpose(mod_s, (1, 0, 2))[None]
        xp, kv_p, cp = _layer(xp, mod_p, conv_zero, None, kv_p, lam_rows, slopes, w, layer=l,
                              n_layers=n_layers, batch=bp, seq=tp, per_row=False,
                              n_heads=n_heads, tm=1024)
        xs, kv_s, cs = _layer(xs, mod_s, state_conv[l], (past_k, past_v, l), kv_s, lam_rows,
                              slopes, w, layer=l, n_layers=n_layers, batch=bs, seq=ts,
                              per_row=True, n_heads=n_heads, tm=512)
        conv_p.append(cp)
        conv_s.append(cs)

    y_prompt = _final_norm(xp, g_final, tm=512).reshape(bp, tp, d)
    y_sample = _final_norm(xs, g_final, tm=512).reshape(bs, ts, d)
    kv_shape_p = (n_layers, bp, tp, n_heads, HEAD_WIDTH)
    kv_shape_s = (n_layers, bs, ts, n_heads, HEAD_WIDTH)
    return (y_prompt, y_sample,
            kv_p[0].reshape(kv_shape_p), kv_p[1].reshape(kv_shape_p), jnp.stack(conv_p),
            kv_s[0].reshape(kv_shape_s), kv_s[1].reshape(kv_shape_s), jnp.stack(conv_s))
```

```python
import functools
import math

import jax
import jax.numpy as jnp
from jax import lax
from jax.experimental import pallas as pl
from jax.experimental.pallas import tpu as pltpu

CHUNK = 64
CONV_WIDTH = 3
CONV_HALO = CONV_WIDTH - 1
EPS = 1e-6
NEG_INF = -1e30
HEAD_DIM = 128
HEAD_WIDTH = 2 * HEAD_DIM
LOG2E = math.log2(math.e)
LANES = 128

V7X_VMEM_BYTES = 64 * 1024 * 1024
VMEM_HEADROOM_BYTES = 8 * 1024 * 1024

BF16 = jnp.bfloat16
F32 = jnp.float32


def _vmem_limit(*block_bytes, scratch=0):
    need = 2 * sum(block_bytes) + scratch + VMEM_HEADROOM_BYTES
    return int(min(need, V7X_VMEM_BYTES - 4 * 1024 * 1024))


def _nbytes(shape, dtype):
    return math.prod(shape) * jnp.dtype(dtype).itemsize


def _mod_spec(per_row, which, tm, tn, tiles_per_batch, col_from_j):
    if per_row:
        return pl.BlockSpec((None, None, tm, tn),
                            lambda i, j: (0, which, i, j if col_from_j else 0))
    return pl.BlockSpec((None, None, 1, tn),
                        lambda i, j: (i // tiles_per_batch, which, 0, j if col_from_j else 0))


def _normmod(x, g, scale, shift):
    ms = jnp.mean(x * x, axis=-1, keepdims=True)
    return (x * lax.rsqrt(ms + EPS)) * g * (1.0 + scale) + shift


def _mod_kernel(c_ref, w_ref, b_ref, o_ref):
    c = c_ref[...]
    a = (c * jax.nn.sigmoid(c)).astype(BF16)
    y = jnp.dot(a, w_ref[...].astype(BF16), preferred_element_type=F32)
    o_ref[...] = y + b_ref[...]


def _modulation(c_all, w_ada, b_ada):
    n_layers, d, d6 = w_ada.shape
    rows = c_all.shape[0]
    tn = 1024
    return pl.pallas_call(
        _mod_kernel,
        out_shape=jax.ShapeDtypeStruct((n_layers, rows, d6), F32),
        grid=(n_layers, d6 // tn),
        in_specs=[
            pl.BlockSpec((rows, d), lambda l, j: (0, 0)),
            pl.BlockSpec((None, d, tn), lambda l, j: (l, 0, j)),
            pl.BlockSpec((None, 1, tn), lambda l, j: (l, 0, j)),
        ],
        out_specs=pl.BlockSpec((None, rows, tn), lambda l, j: (l, 0, j)),
        compiler_params=pltpu.CompilerParams(
            dimension_semantics=("arbitrary", "arbitrary"),
            vmem_limit_bytes=_vmem_limit(_nbytes((d, tn), F32), _nbytes((rows, d), F32),
                                         scratch=_nbytes((d, tn), BF16))),
        name="modulation",
    )(c_all, w_ada, b_ada.reshape(n_layers, 1, d6))


def _normmod_kernel(x_ref, g_ref, sc_ref, sh_ref, h_ref):
    h = _normmod(x_ref[...], g_ref[...], sc_ref[...], sh_ref[...])
    h_ref[...] = h.astype(h_ref.dtype)


def _norm_modulate(x, g, mod, *, per_row, tokens_per_batch, tm):
    n, d = x.shape
    tm = min(tm, n)
    tiles_per_batch = max(tokens_per_batch // tm, 1)
    return pl.pallas_call(
        _normmod_kernel,
        out_shape=jax.ShapeDtypeStruct((n, d), BF16),
        grid=(n // tm, 1),
        in_specs=[
            pl.BlockSpec((tm, d), lambda i, j: (i, 0)),
            pl.BlockSpec((1, d), lambda i, j: (0, 0)),
            _mod_spec(per_row, 1, tm, d, tiles_per_batch, False),
            _mod_spec(per_row, 0, tm, d, tiles_per_batch, False),
        ],
        out_specs=pl.BlockSpec((tm, d), lambda i, j: (i, 0)),
        compiler_params=pltpu.CompilerParams(
            dimension_semantics=("parallel", "arbitrary"),
            vmem_limit_bytes=_vmem_limit(
                _nbytes((tm, d), F32), _nbytes((tm, d), BF16),
                2 * _nbytes((tm if per_row else 8, d), F32), scratch=2 * _nbytes((tm, d), F32))),
        name="norm_modulate",
    )(x, g.reshape(1, d), mod, mod)


def _project_kernel(h_ref, w_ref, o_ref, *, scale):
    y = jnp.dot(h_ref[...], w_ref[...], preferred_element_type=F32)
    if scale != 1.0:
        y = y * scale
    o_ref[...] = y.astype(o_ref.dtype)


def _project(h, w, *, col_start, n_cols, scale=1.0, tm, tn, name):
    n, d = h.shape
    tm = min(tm, n)
    col_blk0 = col_start // tn
    kern = functools.partial(_project_kernel, scale=scale)
    return pl.pallas_call(
        kern,
        out_shape=jax.ShapeDtypeStruct((n, n_cols), BF16),
        grid=(n // tm, n_cols // tn),
        in_specs=[
            pl.BlockSpec((tm, d), lambda i, j: (i, 0)),
            pl.BlockSpec((d, tn), lambda i, j: (0, col_blk0 + j)),
        ],
        out_specs=pl.BlockSpec((tm, tn), lambda i, j: (i, j)),
        compiler_params=pltpu.CompilerParams(
            dimension_semantics=("parallel", "arbitrary"),
            vmem_limit_bytes=_vmem_limit(
                _nbytes((tm, d), BF16), _nbytes((d, tn), BF16), _nbytes((tm, tn), BF16),
                scratch=2 * _nbytes((tm, tn), F32))),
        name=name,
    )(h, w)


def _project_kv_kernel(*refs, n_heads, n_carried):
    h_ref, w_ref = refs[:2]
    bf_ref, head_major_ref = refs[2 + n_carried:]
    y = jnp.dot(h_ref[...], w_ref[...], preferred_element_type=F32)
    bf_ref[...] = y.astype(bf_ref.dtype)
    for hd in range(n_heads):
        head_major_ref[:, hd, :] = y[:, hd * HEAD_WIDTH:(hd + 1) * HEAD_WIDTH]


def _project_kv(h, w, carried, *, col_start, layer, n_layers, n_heads, tm):
    n, d = h.shape
    tm = min(tm, n)
    carried = () if carried is None else (carried,)
    kern = functools.partial(_project_kv_kernel, n_heads=n_heads, n_carried=len(carried))
    return pl.pallas_call(
        kern,
        out_shape=(jax.ShapeDtypeStruct((n, d), BF16),
                   jax.ShapeDtypeStruct((n_layers, n, n_heads, HEAD_WIDTH), F32)),
        grid=(n // tm,),
        in_specs=[
            pl.BlockSpec((tm, d), lambda i: (i, 0)),
            pl.BlockSpec((d, d), lambda i: (0, col_start // d)),
        ] + [pl.BlockSpec(memory_space=pl.ANY)] * len(carried),
        out_specs=(pl.BlockSpec((tm, d), lambda i: (i, 0)),
                   pl.BlockSpec((None, tm, n_heads, HEAD_WIDTH), lambda i: (layer, i, 0, 0))),
        input_output_aliases={2: 1} if carried else {},
        compiler_params=pltpu.CompilerParams(
            dimension_semantics=("arbitrary",),
            vmem_limit_bytes=_vmem_limit(
                _nbytes((tm, d), BF16), _nbytes((d, d), BF16), _nbytes((tm, d), BF16),
                _nbytes((tm, d), F32), scratch=_nbytes((tm, d), F32))),
        name="in_proj_kv",
    )(h, w, *carried)


_CONV_PAD = 8


def _causal_conv(u, gb, w, upad_ref, rows):
    upad_ref[_CONV_PAD:_CONV_PAD + rows, :] = u
    yc = w[0:1, :] * upad_ref[_CONV_PAD - 2:_CONV_PAD - 2 + rows, :]
    yc = yc + w[1:2, :] * upad_ref[_CONV_PAD - 1:_CONV_PAD - 1 + rows, :]
    yc = yc + w[2:3, :] * u
    return gb * yc


def _conv_kernel(xin_ref, gc_ref, gb_ref, prev_ref, w_ref, z_ref, state_ref, upad_ref, *, tt):
    t = pl.program_id(1)

    @pl.when(t == 0)
    def _():
        upad_ref[_CONV_PAD - CONV_HALO:_CONV_PAD, :] = prev_ref[...]

    u = gc_ref[...].astype(F32) * xin_ref[...].astype(F32)
    z = _causal_conv(u, gb_ref[...].astype(F32), w_ref[...], upad_ref, tt)
    z_ref[...] = z.astype(z_ref.dtype)
    tail = upad_ref[_CONV_PAD + tt - CONV_HALO:_CONV_PAD + tt, :]
    state_ref[...] = tail
    upad_ref[_CONV_PAD - CONV_HALO:_CONV_PAD, :] = tail


def _gated_conv(xgb, conv_prev, conv_w, *, batch, seq, tt):
    n = xgb.shape[0]
    d = xgb.shape[1] // 3
    tt = min(tt, seq)
    nt = seq // tt
    kern = functools.partial(_conv_kernel, tt=tt)
    row = lambda b, t: b * nt + t
    return pl.pallas_call(
        kern,
        out_shape=(jax.ShapeDtypeStruct((n, d), BF16),
                   jax.ShapeDtypeStruct((batch, CONV_HALO, d), F32)),
        grid=(batch, nt),
        in_specs=[
            pl.BlockSpec((tt, d), lambda b, t: (row(b, t), 0)),
            pl.BlockSpec((tt, d), lambda b, t: (row(b, t), 1)),
            pl.BlockSpec((tt, d), lambda b, t: (row(b, t), 2)),
            pl.BlockSpec((None, CONV_HALO, d), lambda b, t: (b, 0, 0)),
            pl.BlockSpec((CONV_WIDTH, d), lambda b, t: (0, 0)),
        ],
        out_specs=(pl.BlockSpec((tt, d), lambda b, t: (row(b, t), 0)),
                   pl.BlockSpec((None, CONV_HALO, d), lambda b, t: (b, 0, 0))),
        scratch_shapes=[pltpu.VMEM((_CONV_PAD + tt, d), F32)],
        compiler_params=pltpu.CompilerParams(
            dimension_semantics=("parallel", "arbitrary"),
            vmem_limit_bytes=_vmem_limit(4 * _nbytes((tt, d), BF16),
                                         scratch=3 * _nbytes((_CONV_PAD + tt, d), F32))),
        name="gated_conv",
    )(xgb, xgb, xgb, conv_prev, conv_w)


def _inproj_conv_kernel(h_ref, wx_ref, wc_ref, wb_ref, prev_ref, cw_ref,
                        z_ref, state_ref, upad_ref, carry_ref, *, tm, tiles_per_batch):
    i = pl.program_id(0)
    j = pl.program_id(1)

    first = (i % tiles_per_batch) == 0

    @pl.when(first)
    def _():
        upad_ref[_CONV_PAD - CONV_HALO:_CONV_PAD, :] = prev_ref[...]

    @pl.when(jnp.logical_not(first))
    def _():
        upad_ref[_CONV_PAD - CONV_HALO:_CONV_PAD, :] = carry_ref[j]

    h = h_ref[...]
    xin = jnp.dot(h, wx_ref[...], preferred_element_type=F32)
    gc = jnp.dot(h, wc_ref[...], preferred_element_type=F32)
    gb = jnp.dot(h, wb_ref[...], preferred_element_type=F32)
    z = _causal_conv(gc * xin, gb, cw_ref[...], upad_ref, tm)
    z_ref[...] = z.astype(z_ref.dtype)
    tail = upad_ref[_CONV_PAD + tm - CONV_HALO:_CONV_PAD + tm, :]
    state_ref[...] = tail
    carry_ref[j] = tail


def _inproj_conv(h, w_in, conv_prev, conv_w, *, tokens_per_batch, tm, tn=512):
    n, d = h.shape
    assert tokens_per_batch % tm == 0
    tiles_per_batch = tokens_per_batch // tm
    nb = d // tn
    batch = n // tokens_per_batch
    kern = functools.partial(_inproj_conv_kernel, tm=tm, tiles_per_batch=tiles_per_batch)
    w_spec = lambda seg: pl.BlockSpec((d, tn), lambda i, j: (0, seg * nb + j))
    z, tails = pl.pallas_call(
        kern,
        out_shape=(jax.ShapeDtypeStruct((n, d), BF16),
                   jax.ShapeDtypeStruct((n // tm, CONV_HALO, d), F32)),
        grid=(n // tm, nb),
        in_specs=[
            pl.BlockSpec((tm, d), lambda i, j: (i, 0)),
            w_spec(0), w_spec(1), w_spec(2),
            pl.BlockSpec((None, CONV_HALO, tn), lambda i, j: (i // tiles_per_batch, 0, j)),
            pl.BlockSpec((CONV_WIDTH, tn), lambda i, j: (0, j)),
        ],
        out_specs=(pl.BlockSpec((tm, tn), lambda i, j: (i, j)),
                   pl.BlockSpec((None, CONV_HALO, tn), lambda i, j: (i, 0, j))),
        scratch_shapes=[pltpu.VMEM((_CONV_PAD + tm, tn), F32),
                        pltpu.VMEM((nb, CONV_HALO, tn), F32)],
        compiler_params=pltpu.CompilerParams(
            dimension_semantics=("arbitrary", "arbitrary"),
            vmem_limit_bytes=_vmem_limit(
                _nbytes((tm, d), BF16), 3 * _nbytes((d, tn), BF16), _nbytes((tm, tn), BF16),
                scratch=6 * _nbytes((tm, tn), F32))),
        name="in_proj_conv",
    )(h, w_in, w_in, w_in, conv_prev, conv_w)
    return z, tails.reshape(batch, tiles_per_batch, CONV_HALO, d)[:, -1]


def _lambda_value(lam_ref):
    e1 = jnp.exp(jnp.sum(lam_ref[0:1, :] * lam_ref[1:2, :], axis=-1, keepdims=True))
    e2 = jnp.exp(jnp.sum(lam_ref[2:3, :] * lam_ref[3:4, :], axis=-1, keepdims=True))
    lam_init = lam_ref[4:5, 0:1]
    return e1 - e2 + lam_init, lam_init


def _widen(stat, width):
    if width < LANES:
        return stat[:, :width]
    return jnp.tile(stat, (1, width // LANES))


def _softmax_block_update(s, v, m_ref, l_ref, acc_ref, idx):
    m_prev = m_ref[idx]
    m_new = jnp.maximum(m_prev, jnp.max(s, axis=-1, keepdims=True))
    alpha = jnp.exp2(m_prev - m_new)
    p = jnp.exp2(s - _widen(m_new, s.shape[1]))
    l_ref[idx] = alpha * l_ref[idx] + jnp.sum(p, axis=-1, keepdims=True)
    pv = jnp.dot(p.astype(BF16), v, preferred_element_type=F32)
    acc_ref[idx] = _widen(alpha, HEAD_WIDTH) * acc_ref[idx] + pv
    m_ref[idx] = m_new


def _diff_head_output(acc_ref, l_ref, i1, i2, lam, lam_init, g_subln):
    o = (acc_ref[i1] / _widen(l_ref[i1], HEAD_WIDTH)
         - lam * (acc_ref[i2] / _widen(l_ref[i2], HEAD_WIDTH)))
    ms = jnp.mean(o * o, axis=-1, keepdims=True)
    return (o * lax.rsqrt(ms + EPS)) * g_subln * (1.0 - lam_init)


def _qk_scores(q, k):
    return lax.dot_general(q, k, (((1,), (1,)), ((), ())), preferred_element_type=F32)


def _init_softmax_state(m_ref, l_ref, acc_ref):
    m_ref[...] = jnp.full(m_ref.shape, NEG_INF, F32)
    l_ref[...] = jnp.zeros(l_ref.shape, F32)
    acc_ref[...] = jnp.zeros(acc_ref.shape, F32)


def _attn_kernel(q_ref, k_ref, v_ref, lam_ref, slope_ref, gs_ref, o_ref,
                 m_ref, l_ref, acc_ref, dbias_ref, *, tile):
    qi = pl.program_id(2)
    slope = slope_ref[0:1, 0:1]

    @pl.when(qi == 0)
    def _():
        t = lax.broadcasted_iota(jnp.int32, (tile, tile), 0)
        s = lax.broadcasted_iota(jnp.int32, (tile, tile), 1)
        bias = slope * (t - jnp.abs(t - s)).astype(F32)
        dbias_ref[...] = jnp.where((s // CHUNK) <= (t // CHUNK), bias, NEG_INF)

    _init_softmax_state(m_ref, l_ref, acc_ref)
    q = q_ref[...]

    def visit(ki, add_bias):
        k0 = pl.multiple_of(ki * tile, tile)
        k = k_ref[pl.ds(k0, tile), :]
        v = v_ref[pl.ds(k0, tile), :]
        for c in range(2):
            lanes = slice(c * HEAD_DIM, (c + 1) * HEAD_DIM)
            s = add_bias(_qk_scores(q[:, lanes], k[:, lanes]))
            _softmax_block_update(s, v, m_ref, l_ref, acc_ref, c)

    def earlier_tile(ki):
        rel = lax.broadcasted_iota(jnp.int32, (1, tile), 1) + (ki - qi) * tile
        col_bias = slope * rel.astype(F32)
        visit(ki, lambda s: s + col_bias)

    def earlier_pair(kp, carry):
        earlier_tile(2 * kp)
        earlier_tile(2 * kp + 1)
        return carry

    lax.fori_loop(0, qi // 2, earlier_pair, 0)

    @pl.when(qi % 2 == 1)
    def _():
        earlier_tile(qi - 1)

    visit(qi, lambda s: s + dbias_ref[...])

    lam, lam_init = _lambda_value(lam_ref)
    o = _diff_head_output(acc_ref, l_ref, 0, 1, lam, lam_init, gs_ref[...])
    o_ref[...] = o.astype(o_ref.dtype)


def _attention_prompt(q, k, v, lam_rows, slopes, g_subln, *, batch, seq, n_heads, tile):
    n, d = q.shape
    tile = min(tile, seq)
    assert seq % tile == 0 and tile % CHUNK == 0
    nq = seq // tile
    kern = functools.partial(_attn_kernel, tile=tile)
    kv_spec = pl.BlockSpec((seq, HEAD_WIDTH), lambda b, h, i: (b, h))
    return pl.pallas_call(
        kern,
        out_shape=jax.ShapeDtypeStruct((n, d), BF16),
        grid=(batch, n_heads, nq),
        in_specs=[
            pl.BlockSpec((tile, HEAD_WIDTH), lambda b, h, i: (b * nq + i, h)),
            kv_spec,
            kv_spec,
            pl.BlockSpec((8, HEAD_DIM), lambda b, h, i: (0, 0)),
            pl.BlockSpec((None, 1, HEAD_DIM), lambda b, h, i: (h, 0, 0)),
            pl.BlockSpec((1, HEAD_WIDTH), lambda b, h, i: (0, 0)),
        ],
        out_specs=pl.BlockSpec((tile, HEAD_WIDTH), lambda b, h, i: (b * nq + i, h)),
        scratch_shapes=[
            pltpu.VMEM((2, tile, LANES), F32),
            pltpu.VMEM((2, tile, LANES), F32),
            pltpu.VMEM((2, tile, HEAD_WIDTH), F32),
            pltpu.VMEM((tile, tile), F32),
        ],
        compiler_params=pltpu.CompilerParams(
            dimension_semantics=("parallel", "parallel", "arbitrary"),
            vmem_limit_bytes=_vmem_limit(
                2 * _nbytes((seq, HEAD_WIDTH), BF16), 2 * _nbytes((tile, HEAD_WIDTH), BF16),
                scratch=4 * _nbytes((tile, LANES), F32) + 2 * _nbytes((tile, HEAD_WIDTH), F32)
                + 9 * _nbytes((tile, tile), F32))),
        name="diff_attention_prompt",
    )(q, k, v, lam_rows, slopes, g_subln.reshape(1, HEAD_WIDTH))


def _attn_decode_kernel(q_ref, kp_ref, vp_ref, kn_ref, vn_ref, lam_ref, gs_ref, o_ref,
                        m_ref, l_ref, acc_ref, *, tk, past, seq, n_heads):
    ki = pl.program_id(1)

    @pl.when(ki == 0)
    def _():
        _init_softmax_state(m_ref, l_ref, acc_ref)

    def process(k_ref, v_ref, kpos0, n_keys):
        qpos = lax.broadcasted_iota(jnp.int32, (seq, n_keys), 0) + past
        kpos = lax.broadcasted_iota(jnp.int32, (seq, n_keys), 1) + kpos0
        dist = jnp.abs(qpos - kpos).astype(F32)
        allowed = (kpos // CHUNK) <= (qpos // CHUNK)
        for h in range(n_heads):
            slope = LOG2E * 2.0 ** (-8.0 * (h + 1) / n_heads)
            bias = -slope * dist
            v = v_ref[:, h * HEAD_WIDTH:(h + 1) * HEAD_WIDTH].astype(BF16)
            for c in range(2):
                lanes = slice(h * HEAD_WIDTH + c * HEAD_DIM, h * HEAD_WIDTH + (c + 1) * HEAD_DIM)
                s = _qk_scores(q_ref[:, lanes], k_ref[:, lanes].astype(BF16)) + bias
                s = jnp.where(allowed, s, NEG_INF)
                _softmax_block_update(s, v, m_ref, l_ref, acc_ref, 2 * h + c)

    process(kp_ref, vp_ref, ki * tk, tk)

    @pl.when(ki == pl.num_programs(1) - 1)
    def _():
        process(kn_ref, vn_ref, past, seq)
        lam, lam_init = _lambda_value(lam_ref)
        for h in range(n_heads):
            o = _diff_head_output(acc_ref, l_ref, 2 * h, 2 * h + 1, lam, lam_init, gs_ref[...])
            o_ref[:, h * HEAD_WIDTH:(h + 1) * HEAD_WIDTH] = o.astype(o_ref.dtype)


def _attention_decode(q, k_new, v_new, k_past, v_past, layer, lam_rows, g_subln, *, batch, seq,
                      n_heads, tk):
    n, d = q.shape
    past = k_past.shape[1]
    tk = min(tk, past)
    kern = functools.partial(_attn_decode_kernel, tk=tk, past=past, seq=seq, n_heads=n_heads)
    past_spec = pl.BlockSpec((None, tk, d), lambda b, j: (layer * batch + b, j, 0))
    new_spec = pl.BlockSpec((seq, d), lambda b, j: (b, 0))
    return pl.pallas_call(
        kern,
        out_shape=jax.ShapeDtypeStruct((n, d), BF16),
        grid=(batch, past // tk),
        in_specs=[
            new_spec, past_spec, past_spec, new_spec, new_spec,
            pl.BlockSpec((8, HEAD_DIM), lambda b, j: (0, 0)),
            pl.BlockSpec((1, HEAD_WIDTH), lambda b, j: (0, 0)),
        ],
        out_specs=new_spec,
        scratch_shapes=[
            pltpu.VMEM((2 * n_heads, seq, LANES), F32),
            pltpu.VMEM((2 * n_heads, seq, LANES), F32),
            pltpu.VMEM((2 * n_heads, seq, HEAD_WIDTH), F32),
        ],
        compiler_params=pltpu.CompilerParams(
            dimension_semantics=("parallel", "arbitrary"),
            vmem_limit_bytes=_vmem_limit(
                2 * _nbytes((tk, d), F32), 4 * _nbytes((seq, d), F32),
                scratch=4 * n_heads * _nbytes((seq, LANES), F32)
                + 2 * n_heads * _nbytes((seq, HEAD_WIDTH), F32) + _nbytes((tk, d), F32))),
        name="diff_attention_decode",
    )(q, k_past, v_past, k_new, v_new, lam_rows, g_subln.reshape(1, HEAD_WIDTH))


def _merge_kernel(z_ref, o_ref, gc_ref, ga_ref, wc_ref, wa_ref, out_ref):
    y_conv = jnp.dot(z_ref[...], wc_ref[...], preferred_element_type=F32)
    y_attn = jnp.dot(o_ref[...], wa_ref[...], preferred_element_type=F32)
    merged = (jax.nn.sigmoid(gc_ref[...].astype(F32)) * y_conv
              + jax.nn.sigmoid(ga_ref[...].astype(F32)) * y_attn)
    out_ref[...] = merged.astype(out_ref.dtype)


def _merge_branches(z, o, gates, w_conv_out, w_attn_out, *, tm, tn=512):
    n, d = z.shape
    tm = min(tm, n)
    gate_blocks = d // tn
    return pl.pallas_call(
        _merge_kernel,
        out_shape=jax.ShapeDtypeStruct((n, d), BF16),
        grid=(n // tm, d // tn),
        in_specs=[
            pl.BlockSpec((tm, d), lambda i, j: (i, 0)),
            pl.BlockSpec((tm, d), lambda i, j: (i, 0)),
            pl.BlockSpec((tm, tn), lambda i, j: (i, j)),
            pl.BlockSpec((tm, tn), lambda i, j: (i, gate_blocks + j)),
            pl.BlockSpec((d, tn), lambda i, j: (0, j)),
            pl.BlockSpec((d, tn), lambda i, j: (0, j)),
        ],
        out_specs=pl.BlockSpec((tm, tn), lambda i, j: (i, j)),
        compiler_params=pltpu.CompilerParams(
            dimension_semantics=("parallel", "arbitrary"),
            vmem_limit_bytes=_vmem_limit(
                2 * _nbytes((tm, d), BF16), 2 * _nbytes((d, tn), BF16),
                3 * _nbytes((tm, tn), BF16), scratch=4 * _nbytes((tm, tn), F32))),
        name="merge_branches",
    )(z, o, gates, gates, w_conv_out, w_attn_out)


def _proj_residual_kernel(a_ref, w_ref, x_ref, gate_ref, o_ref):
    y = jnp.dot(a_ref[...], w_ref[...], preferred_element_type=F32)
    o_ref[...] = x_ref[...] + gate_ref[...] * y


def _proj_residual(a, w, x, mod, *, which_gate, per_row, tokens_per_batch, tm, tn=1024):
    n, d = x.shape
    tm = min(tm, n)
    tiles_per_batch = max(tokens_per_batch // tm, 1)
    return pl.pallas_call(
        _proj_residual_kernel,
        out_shape=jax.ShapeDtypeStruct((n, d), F32),
        grid=(n // tm, d // tn),
        in_specs=[
            pl.BlockSpec((tm, a.shape[1]), lambda i, j: (i, 0)),
            pl.BlockSpec((a.shape[1], tn), lambda i, j: (0, j)),
            pl.BlockSpec((tm, tn), lambda i, j: (i, j)),
            _mod_spec(per_row, which_gate, tm, tn, tiles_per_batch, True),
        ],
        out_specs=pl.BlockSpec((tm, tn), lambda i, j: (i, j)),
        compiler_params=pltpu.CompilerParams(
            dimension_semantics=("parallel", "arbitrary"),
            vmem_limit_bytes=_vmem_limit(
                _nbytes((tm, a.shape[1]), BF16), _nbytes((a.shape[1], tn), BF16),
                3 * _nbytes((tm, tn), F32), scratch=_nbytes((tm, tn), F32))),
        name="proj_residual",
    )(a, w, x, mod)


def _mlp_kernel(x_ref, g_ref, sc_ref, sh_ref, gate_ref, wup_ref, wdn_ref, o_ref, h_ref, acc_ref):
    kf = pl.program_id(1)

    @pl.when(kf == 0)
    def _():
        h = _normmod(x_ref[...], g_ref[...], sc_ref[...], sh_ref[...])
        h_ref[...] = h.astype(h_ref.dtype)
        acc_ref[...] = jnp.zeros(acc_ref.shape, F32)

    up = jnp.dot(h_ref[...], wup_ref[...], preferred_element_type=F32)
    act = jnp.square(jnp.maximum(up, 0.0)).astype(BF16)
    acc_ref[...] += jnp.dot(act, wdn_ref[...], preferred_element_type=F32)

    @pl.when(kf == pl.num_programs(1) - 1)
    def _():
        o_ref[...] = x_ref[...] + gate_ref[...] * acc_ref[...]


def _mlp(x, g, mod, w_up, w_down, *, per_row, tokens_per_batch, tm, tf=1024):
    n, d = x.shape
    d_ff = w_up.shape[1]
    tm = min(tm, n)
    tiles_per_batch = max(tokens_per_batch // tm, 1)
    return pl.pallas_call(
        _mlp_kernel,
        out_shape=jax.ShapeDtypeStruct((n, d), F32),
        grid=(n // tm, d_ff // tf),
        in_specs=[
            pl.BlockSpec((tm, d), lambda i, j: (i, 0)),
            pl.BlockSpec((1, d), lambda i, j: (0, 0)),
            _mod_spec(per_row, 4, tm, d, tiles_per_batch, False),
            _mod_spec(per_row, 3, tm, d, tiles_per_batch, False),
            _mod_spec(per_row, 5, tm, d, tiles_per_batch, False),
            pl.BlockSpec((d, tf), lambda i, j: (0, j)),
            pl.BlockSpec((tf, d), lambda i, j: (j, 0)),
        ],
        out_specs=pl.BlockSpec((tm, d), lambda i, j: (i, 0)),
        scratch_shapes=[pltpu.VMEM((tm, d), BF16), pltpu.VMEM((tm, d), F32)],
        compiler_params=pltpu.CompilerParams(
            dimension_semantics=("parallel", "arbitrary"),
            vmem_limit_bytes=_vmem_limit(
                2 * _nbytes((tm, d), F32), 2 * _nbytes((d, tf), BF16),
                3 * _nbytes((tm if per_row else 8, d), F32),
                scratch=_nbytes((tm, d), BF16) + _nbytes((tm, d), F32)
                + _nbytes((tm, tf), F32))),
        name="mlp",
    )(x, g.reshape(1, d), mod, mod, mod, w_up, w_down)


def _final_norm_kernel(x_ref, g_ref, o_ref):
    x = x_ref[...]
    ms = jnp.mean(x * x, axis=-1, keepdims=True)
    o_ref[...] = (x * lax.rsqrt(ms + EPS)) * g_ref[...]


def _final_norm(x, g, *, tm):
    n, d = x.shape
    tm = min(tm, n)
    return pl.pallas_call(
        _final_norm_kernel,
        out_shape=jax.ShapeDtypeStruct((n, d), F32),
        grid=(n // tm,),
        in_specs=[pl.BlockSpec((tm, d), lambda i: (i, 0)), pl.BlockSpec((1, d), lambda i: (0, 0))],
        out_specs=pl.BlockSpec((tm, d), lambda i: (i, 0)),
        compiler_params=pltpu.CompilerParams(
            dimension_semantics=("parallel",),
            vmem_limit_bytes=_vmem_limit(2 * _nbytes((tm, d), F32))),
        name="final_norm",
    )(x, g.reshape(1, d))


def _layer(x, mod, conv_prev, kv_past, kv_all, lam_rows, slopes, w, *, layer, n_layers, batch, seq,
           per_row, n_heads, tm):
    g_mix, w_in, conv_w, g_subln, w_conv_out, w_attn_out, w_o, g_mlp, w_up, w_down = w
    n, d = x.shape
    h = _norm_modulate(x, g_mix, mod, per_row=per_row, tokens_per_batch=seq, tm=512)
    if seq % tm == 0:
        z, new_conv = _inproj_conv(h, w_in, conv_prev, conv_w, tokens_per_batch=seq, tm=tm)
    else:
        xgb = _project(h, w_in, col_start=0, n_cols=3 * d, tm=tm, tn=1024,
                       name="in_proj_conv_inputs")
        z, new_conv = _gated_conv(xgb, conv_prev, conv_w, batch=batch, seq=seq, tt=512)
    q = _project(h, w_in, col_start=3 * d, n_cols=d, scale=LOG2E * HEAD_DIM ** -0.5, tm=tm,
                 tn=d, name="in_proj_q")
    k_all, v_all = (None, None) if kv_all is None else kv_all
    k_bf, k_all = _project_kv(h, w_in, k_all, col_start=4 * d, layer=layer, n_layers=n_layers,
                              n_heads=n_heads, tm=512)
    v_bf, v_all = _project_kv(h, w_in, v_all, col_start=5 * d, layer=layer, n_layers=n_layers,
                              n_heads=n_heads, tm=512)
    gates = _project(h, w_in, col_start=6 * d, n_cols=2 * d, tm=tm, tn=d, name="in_proj_gates")

    if kv_past is None:
        o = _attention_prompt(q, k_bf, v_bf, lam_rows, slopes, g_subln, batch=batch, seq=seq,
                              n_heads=n_heads, tile=512)
    else:
        o = _attention_decode(q, k_bf, v_bf, *kv_past, lam_rows, g_subln,
                              batch=batch, seq=seq, n_heads=n_heads, tk=512)

    merged = _merge_branches(z, o, gates, w_conv_out, w_attn_out, tm=tm)
    x = _proj_residual(merged, w_o, x, mod, which_gate=2, per_row=per_row,
                       tokens_per_batch=seq, tm=tm)
    x = _mlp(x, g_mlp, mod, w_up, w_down, per_row=per_row, tokens_per_batch=seq, tm=512)
    return x, (k_all, v_all), new_conv


def kernel(x_prompt, x_sample, c_prompt, c_sample, cache_k, cache_v, state_conv, w_ada, b_ada, g_mix, w_in, conv_w, lambda_q1, lambda_k1, lambda_q2, lambda_k2, g_subln, w_conv_out, w_attn_out, w_o, g_mlp, w_up, w_down, g_final):
    bp, tp, d = x_prompt.shape
    bs, ts, _ = x_sample.shape
    n_layers = w_in.shape[0]
    n_heads = cache_k.shape[3]

    mod_all = _modulation(jnp.concatenate([c_prompt, c_sample], axis=0), w_ada, b_ada)
    slopes = jnp.broadcast_to(
        LOG2E * jnp.exp2(-8.0 * (jnp.arange(n_heads, dtype=F32) + 1.0) / n_heads)[:, None, None],
        (n_heads, 1, HEAD_DIM))

    xp = x_prompt.reshape(bp * tp, d)
    xs = x_sample.reshape(bs * ts, d)
    conv_zero = jnp.zeros((bp, CONV_HALO, d), F32)
    past_len = cache_k.shape[2]
    past_k = cache_k.reshape(n_layers * bs, past_len, d)
    past_v = cache_v.reshape(n_layers * bs, past_len, d)
    conv_p, conv_s = [], []
    kv_p = kv_s = None
    for l in range(n_layers):
        lam_init = 0.8 - 0.6 * math.exp(-0.3 * l)
        lam_rows = jnp.concatenate(
            [lambda_q1[l][None], lambda_k1[l][None], lambda_q2[l][None], lambda_k2[l][None],
             jnp.full((4, HEAD_DIM), lam_init, F32)], axis=0).astype(F32)
        w = (g_mix[l], w_in[l].astype(BF16), conv_w[l], g_subln[l], w_conv_out[l].astype(BF16),
             w_attn_out[l].astype(BF16), w_o[l].astype(BF16), g_mlp[l], w_up[l].astype(BF16),
             w_down[l].astype(BF16))
        mod_p = mod_all[l, :bp].reshape(bp, 6, 1, d)
        mod_s = jnp.repeat(mod_all[l, bp:].reshape(bs, 6, d), ts, axis=0)
        mod_s = jnp.transpose(mod_s, (1, 0, 2))[None]
        xp, kv_p, cp = _layer(xp, mod_p, conv_zero, None, kv_p, lam_rows, slopes, w, layer=l,
                              n_layers=n_layers, batch=bp, seq=tp, per_row=False,
                              n_heads=n_heads, tm=1024)
        xs, kv_s, cs = _layer(xs, mod_s, state_conv[l], (past_k, past_v, l), kv_s, lam_rows,
                              slopes, w, layer=l, n_layers=n_layers, batch=bs, seq=ts,
                              per_row=True, n_heads=n_heads, tm=512)
        conv_p.append(cp)
        conv_s.append(cs)

    y_prompt = _final_norm(xp, g_final, tm=512).reshape(bp, tp, d)
    y_sample = _final_norm(xs, g_final, tm=512).reshape(bs, ts, d)
    kv_shape_p = (n_layers, bp, tp, n_heads, HEAD_WIDTH)
    kv_shape_s = (n_layers, bs, ts, n_heads, HEAD_WIDTH)
    return (y_prompt, y_sample,
            kv_p[0].reshape(kv_shape_p), kv_p[1].reshape(kv_shape_p), jnp.stack(conv_p),
            kv_s[0].reshape(kv_shape_s), kv_s[1].reshape(kv_shape_s), jnp.stack(conv_s))
```

```python
import functools
import math

import jax
import jax.numpy as jnp
from jax import lax
from jax.experimental import pallas as pl
from jax.experimental.pallas import tpu as pltpu

CHUNK = 64
CONV_WIDTH = 3
CONV_HALO = CONV_WIDTH - 1
EPS = 1e-6
NEG_INF = -1e30
HEAD_DIM = 128
HEAD_WIDTH = 2 * HEAD_DIM
LOG2E = math.log2(math.e)
LANES = 128

V7X_VMEM_BYTES = 64 * 1024 * 1024
VMEM_HEADROOM_BYTES = 8 * 1024 * 1024

BF16 = jnp.bfloat16
F32 = jnp.float32


def _vmem_limit(*block_bytes, scratch=0):
    need = 2 * sum(block_bytes) + scratch + VMEM_HEADROOM_BYTES
    return int(min(need, V7X_VMEM_BYTES - 4 * 1024 * 1024))


def _nbytes(shape, dtype):
    return math.prod(shape) * jnp.dtype(dtype).itemsize


def _mod_spec(per_row, which, tm, tn, tiles_per_batch, col_from_j):
    if per_row:
        return pl.BlockSpec((None, None, tm, tn),
                            lambda i, j: (0, which, i, j if col_from_j else 0))
    return pl.BlockSpec((None, None, 1, tn),
                        lambda i, j: (i // tiles_per_batch, which, 0, j if col_from_j else 0))


def _normmod(x, g, scale, shift):
    ms = jnp.mean(x * x, axis=-1, keepdims=True)
    return (x * lax.rsqrt(ms + EPS)) * g * (1.0 + scale) + shift


def _mod_kernel(c_ref, w_ref, b_ref, o_ref):
    c = c_ref[...]
    a = (c * jax.nn.sigmoid(c)).astype(BF16)
    y = jnp.dot(a, w_ref[...].astype(BF16), preferred_element_type=F32)
    o_ref[...] = y + b_ref[...]


def _modulation(c_all, w_ada, b_ada):
    n_layers, d, d6 = w_ada.shape
    rows = c_all.shape[0]
    tn = 1024
    return pl.pallas_call(
        _mod_kernel,
        out_shape=jax.ShapeDtypeStruct((n_layers, rows, d6), F32),
        grid=(n_layers, d6 // tn),
        in_specs=[
            pl.BlockSpec((rows, d), lambda l, j: (0, 0)),
            pl.BlockSpec((None, d, tn), lambda l, j: (l, 0, j)),
            pl.BlockSpec((None, 1, tn), lambda l, j: (l, 0, j)),
        ],
        out_specs=pl.BlockSpec((None, rows, tn), lambda l, j: (l, 0, j)),
        compiler_params=pltpu.CompilerParams(
            dimension_semantics=("arbitrary", "arbitrary"),
            vmem_limit_bytes=_vmem_limit(_nbytes((d, tn), F32), _nbytes((rows, d), F32),
                                         scratch=_nbytes((d, tn), BF16))),
        name="modulation",
    )(c_all, w_ada, b_ada.reshape(n_layers, 1, d6))


def _normmod_kernel(x_ref, g_ref, sc_ref, sh_ref, h_ref):
    h = _normmod(x_ref[...], g_ref[...], sc_ref[...], sh_ref[...])
    h_ref[...] = h.astype(h_ref.dtype)


def _norm_modulate(x, g, mod, *, per_row, tokens_per_batch, tm):
    n, d = x.shape
    tm = min(tm, n)
    tiles_per_batch = max(tokens_per_batch // tm, 1)
    return pl.pallas_call(
        _normmod_kernel,
        out_shape=jax.ShapeDtypeStruct((n, d), BF16),
        grid=(n // tm, 1),
        in_specs=[
            pl.BlockSpec((tm, d), lambda i, j: (i, 0)),
            pl.BlockSpec((1, d), lambda i, j: (0, 0)),
            _mod_spec(per_row, 1, tm, d, tiles_per_batch, False),
            _mod_spec(per_row, 0, tm, d, tiles_per_batch, False),
        ],
        out_specs=pl.BlockSpec((tm, d), lambda i, j: (i, 0)),
        compiler_params=pltpu.CompilerParams(
            dimension_semantics=("parallel", "arbitrary"),
            vmem_limit_bytes=_vmem_limit(
                _nbytes((tm, d), F32), _nbytes((tm, d), BF16),
                2 * _nbytes((tm if per_row else 8, d), F32), scratch=2 * _nbytes((tm, d), F32))),
        name="norm_modulate",
    )(x, g.reshape(1, d), mod, mod)


def _project_kernel(h_ref, w_ref, o_ref, *, scale):
    y = jnp.dot(h_ref[...], w_ref[...], preferred_element_type=F32)
    if scale != 1.0:
        y = y * scale
    o_ref[...] = y.astype(o_ref.dtype)


def _project(h, w, layer, *, col_start, n_cols, scale=1.0, tm, tn, name):
    n, d = h.shape
    tm = min(tm, n)
    col_blk0 = col_start // tn
    kern = functools.partial(_project_kernel, scale=scale)
    return pl.pallas_call(
        kern,
        out_shape=jax.ShapeDtypeStruct((n, n_cols), BF16),
        grid=(n // tm, n_cols // tn),
        in_specs=[
            pl.BlockSpec((tm, d), lambda i, j: (i, 0)),
            pl.BlockSpec((None, d, tn), lambda i, j: (layer, 0, col_blk0 + j)),
        ],
        out_specs=pl.BlockSpec((tm, tn), lambda i, j: (i, j)),
        compiler_params=pltpu.CompilerParams(
            dimension_semantics=("parallel", "arbitrary"),
            vmem_limit_bytes=_vmem_limit(
                _nbytes((tm, d), BF16), _nbytes((d, tn), BF16), _nbytes((tm, tn), BF16),
                scratch=2 * _nbytes((tm, tn), F32))),
        name=name,
    )(h, w)


def _project_kv_kernel(*refs, n_heads, n_carried):
    h_ref, w_ref = refs[:2]
    bf_ref, head_major_ref = refs[2 + n_carried:]
    y = jnp.dot(h_ref[...], w_ref[...], preferred_element_type=F32)
    bf_ref[...] = y.astype(bf_ref.dtype)
    for hd in range(n_heads):
        head_major_ref[:, hd, :] = y[:, hd * HEAD_WIDTH:(hd + 1) * HEAD_WIDTH]


def _project_kv(h, w, carried, *, col_start, layer, n_layers, n_heads, tm):
    n, d = h.shape
    tm = min(tm, n)
    carried = () if carried is None else (carried,)
    kern = functools.partial(_project_kv_kernel, n_heads=n_heads, n_carried=len(carried))
    return pl.pallas_call(
        kern,
        out_shape=(jax.ShapeDtypeStruct((n, d), BF16),
                   jax.ShapeDtypeStruct((n_layers, n, n_heads, HEAD_WIDTH), F32)),
        grid=(n // tm,),
        in_specs=[
            pl.BlockSpec((tm, d), lambda i: (i, 0)),
            pl.BlockSpec((None, d, d), lambda i: (layer, 0, col_start // d)),
        ] + [pl.BlockSpec(memory_space=pl.ANY)] * len(carried),
        out_specs=(pl.BlockSpec((tm, d), lambda i: (i, 0)),
                   pl.BlockSpec((None, tm, n_heads, HEAD_WIDTH), lambda i: (layer, i, 0, 0))),
        input_output_aliases={2: 1} if carried else {},
        compiler_params=pltpu.CompilerParams(
            dimension_semantics=("arbitrary",),
            vmem_limit_bytes=_vmem_limit(
                _nbytes((tm, d), BF16), _nbytes((d, d), BF16), _nbytes((tm, d), BF16),
                _nbytes((tm, d), F32), scratch=_nbytes((tm, d), F32))),
        name="in_proj_kv",
    )(h, w, *carried)


_CONV_PAD = 8


def _causal_conv(u, gb, w, upad_ref, rows):
    upad_ref[_CONV_PAD:_CONV_PAD + rows, :] = u
    yc = w[0:1, :] * upad_ref[_CONV_PAD - 2:_CONV_PAD - 2 + rows, :]
    yc = yc + w[1:2, :] * upad_ref[_CONV_PAD - 1:_CONV_PAD - 1 + rows, :]
    yc = yc + w[2:3, :] * u
    return gb * yc


def _conv_kernel(xin_ref, gc_ref, gb_ref, prev_ref, w_ref, z_ref, state_ref, upad_ref, *, tt):
    t = pl.program_id(1)

    @pl.when(t == 0)
    def _():
        upad_ref[_CONV_PAD - CONV_HALO:_CONV_PAD, :] = prev_ref[...]

    u = gc_ref[...].astype(F32) * xin_ref[...].astype(F32)
    z = _causal_conv(u, gb_ref[...].astype(F32), w_ref[...], upad_ref, tt)
    z_ref[...] = z.astype(z_ref.dtype)
    tail = upad_ref[_CONV_PAD + tt - CONV_HALO:_CONV_PAD + tt, :]
    state_ref[...] = tail
    upad_ref[_CONV_PAD - CONV_HALO:_CONV_PAD, :] = tail


def _gated_conv(xgb, conv_prev, conv_w, *, batch, seq, tt):
    n = xgb.shape[0]
    d = xgb.shape[1] // 3
    tt = min(tt, seq)
    nt = seq // tt
    kern = functools.partial(_conv_kernel, tt=tt)
    row = lambda b, t: b * nt + t
    return pl.pallas_call(
        kern,
        out_shape=(jax.ShapeDtypeStruct((n, d), BF16),
                   jax.ShapeDtypeStruct((batch, CONV_HALO, d), F32)),
        grid=(batch, nt),
        in_specs=[
            pl.BlockSpec((tt, d), lambda b, t: (row(b, t), 0)),
            pl.BlockSpec((tt, d), lambda b, t: (row(b, t), 1)),
            pl.BlockSpec((tt, d), lambda b, t: (row(b, t), 2)),
            pl.BlockSpec((None, CONV_HALO, d), lambda b, t: (b, 0, 0)),
            pl.BlockSpec((CONV_WIDTH, d), lambda b, t: (0, 0)),
        ],
        out_specs=(pl.BlockSpec((tt, d), lambda b, t: (row(b, t), 0)),
                   pl.BlockSpec((None, CONV_HALO, d), lambda b, t: (b, 0, 0))),
        scratch_shapes=[pltpu.VMEM((_CONV_PAD + tt, d), F32)],
        compiler_params=pltpu.CompilerParams(
            dimension_semantics=("parallel", "arbitrary"),
            vmem_limit_bytes=_vmem_limit(4 * _nbytes((tt, d), BF16),
                                         scratch=3 * _nbytes((_CONV_PAD + tt, d), F32))),
        name="gated_conv",
    )(xgb, xgb, xgb, conv_prev, conv_w)


def _inproj_conv_kernel(h_ref, wx_ref, wc_ref, wb_ref, prev_ref, cw_ref,
                        z_ref, state_ref, upad_ref, carry_ref, *, tm, tiles_per_batch):
    i = pl.program_id(0)
    j = pl.program_id(1)

    first = (i % tiles_per_batch) == 0

    @pl.when(first)
    def _():
        upad_ref[_CONV_PAD - CONV_HALO:_CONV_PAD, :] = prev_ref[...]

    @pl.when(jnp.logical_not(first))
    def _():
        upad_ref[_CONV_PAD - CONV_HALO:_CONV_PAD, :] = carry_ref[j]

    h = h_ref[...]
    xin = jnp.dot(h, wx_ref[...], preferred_element_type=F32)
    gc = jnp.dot(h, wc_ref[...], preferred_element_type=F32)
    gb = jnp.dot(h, wb_ref[...], preferred_element_type=F32)
    z = _causal_conv(gc * xin, gb, cw_ref[...], upad_ref, tm)
    z_ref[...] = z.astype(z_ref.dtype)
    tail = upad_ref[_CONV_PAD + tm - CONV_HALO:_CONV_PAD + tm, :]
    state_ref[...] = tail
    carry_ref[j] = tail


def _inproj_conv(h, w_in, layer, conv_prev, conv_w, *, tokens_per_batch, tm, tn=512):
    n, d = h.shape
    assert tokens_per_batch % tm == 0
    tiles_per_batch = tokens_per_batch // tm
    nb = d // tn
    batch = n // tokens_per_batch
    kern = functools.partial(_inproj_conv_kernel, tm=tm, tiles_per_batch=tiles_per_batch)
    w_spec = lambda seg: pl.BlockSpec((None, d, tn), lambda i, j: (layer, 0, seg * nb + j))
    z, tails = pl.pallas_call(
        kern,
        out_shape=(jax.ShapeDtypeStruct((n, d), BF16),
                   jax.ShapeDtypeStruct((n // tm, CONV_HALO, d), F32)),
        grid=(n // tm, nb),
        in_specs=[
            pl.BlockSpec((tm, d), lambda i, j: (i, 0)),
            w_spec(0), w_spec(1), w_spec(2),
            pl.BlockSpec((None, CONV_HALO, tn), lambda i, j: (i // tiles_per_batch, 0, j)),
            pl.BlockSpec((CONV_WIDTH, tn), lambda i, j: (0, j)),
        ],
        out_specs=(pl.BlockSpec((tm, tn), lambda i, j: (i, j)),
                   pl.BlockSpec((None, CONV_HALO, tn), lambda i, j: (i, 0, j))),
        scratch_shapes=[pltpu.VMEM((_CONV_PAD + tm, tn), F32),
                        pltpu.VMEM((nb, CONV_HALO, tn), F32)],
        compiler_params=pltpu.CompilerParams(
            dimension_semantics=("arbitrary", "arbitrary"),
            vmem_limit_bytes=_vmem_limit(
                _nbytes((tm, d), BF16), 3 * _nbytes((d, tn), BF16), _nbytes((tm, tn), BF16),
                scratch=6 * _nbytes((tm, tn), F32))),
        name="in_proj_conv",
    )(h, w_in, w_in, w_in, conv_prev, conv_w)
    return z, tails.reshape(batch, tiles_per_batch, CONV_HALO, d)[:, -1]


def _lambda_value(lam_ref):
    e1 = jnp.exp(jnp.sum(lam_ref[0:1, :] * lam_ref[1:2, :], axis=-1, keepdims=True))
    e2 = jnp.exp(jnp.sum(lam_ref[2:3, :] * lam_ref[3:4, :], axis=-1, keepdims=True))
    lam_init = lam_ref[4:5, 0:1]
    return e1 - e2 + lam_init, lam_init


def _widen(stat, width):
    if width < LANES:
        return stat[:, :width]
    return jnp.tile(stat, (1, width // LANES))


def _softmax_block_update(s, v, m_ref, l_ref, acc_ref, idx):
    m_prev = m_ref[idx]
    m_new = jnp.maximum(m_prev, jnp.max(s, axis=-1, keepdims=True))
    alpha = jnp.exp2(m_prev - m_new)
    p = jnp.exp2(s - _widen(m_new, s.shape[1]))
    l_ref[idx] = alpha * l_ref[idx] + jnp.sum(p, axis=-1, keepdims=True)
    pv = jnp.dot(p.astype(BF16), v, preferred_element_type=F32)
    acc_ref[idx] = _widen(alpha, HEAD_WIDTH) * acc_ref[idx] + pv
    m_ref[idx] = m_new


def _diff_head_output(acc_ref, l_ref, i1, i2, lam, lam_init, g_subln):
    o = (acc_ref[i1] / _widen(l_ref[i1], HEAD_WIDTH)
         - lam * (acc_ref[i2] / _widen(l_ref[i2], HEAD_WIDTH)))
    ms = jnp.mean(o * o, axis=-1, keepdims=True)
    return (o * lax.rsqrt(ms + EPS)) * g_subln * (1.0 - lam_init)


def _qk_scores(q, k):
    return lax.dot_general(q, k, (((1,), (1,)), ((), ())), preferred_element_type=F32)


def _init_softmax_state(m_ref, l_ref, acc_ref):
    m_ref[...] = jnp.full(m_ref.shape, NEG_INF, F32)
    l_ref[...] = jnp.zeros(l_ref.shape, F32)
    acc_ref[...] = jnp.zeros(acc_ref.shape, F32)


def _attn_kernel(q_ref, k_ref, v_ref, lam_ref, slope_ref, gs_ref, o_ref,
                 m_ref, l_ref, acc_ref, dbias_ref, *, tile):
    qi = pl.program_id(2)
    slope = slope_ref[0:1, 0:1]

    @pl.when(qi == 0)
    def _():
        t = lax.broadcasted_iota(jnp.int32, (tile, tile), 0)
        s = lax.broadcasted_iota(jnp.int32, (tile, tile), 1)
        bias = slope * (t - jnp.abs(t - s)).astype(F32)
        dbias_ref[...] = jnp.where((s // CHUNK) <= (t // CHUNK), bias, NEG_INF)

    _init_softmax_state(m_ref, l_ref, acc_ref)
    q = q_ref[...]

    def visit(ki, add_bias):
        k0 = pl.multiple_of(ki * tile, tile)
        k = k_ref[pl.ds(k0, tile), :]
        v = v_ref[pl.ds(k0, tile), :]
        for c in range(2):
            lanes = slice(c * HEAD_DIM, (c + 1) * HEAD_DIM)
            s = add_bias(_qk_scores(q[:, lanes], k[:, lanes]))
            _softmax_block_update(s, v, m_ref, l_ref, acc_ref, c)

    def earlier_tile(ki):
        rel = lax.broadcasted_iota(jnp.int32, (1, tile), 1) + (ki - qi) * tile
        col_bias = slope * rel.astype(F32)
        visit(ki, lambda s: s + col_bias)

    def earlier_pair(kp, carry):
        earlier_tile(2 * kp)
        earlier_tile(2 * kp + 1)
        return carry

    lax.fori_loop(0, qi // 2, earlier_pair, 0)

    @pl.when(qi % 2 == 1)
    def _():
        earlier_tile(qi - 1)

    visit(qi, lambda s: s + dbias_ref[...])

    lam, lam_init = _lambda_value(lam_ref)
    o = _diff_head_output(acc_ref, l_ref, 0, 1, lam, lam_init, gs_ref[...])
    o_ref[...] = o.astype(o_ref.dtype)


def _attention_prompt(q, k, v, lam_rows, slopes, g_subln, *, batch, seq, n_heads, tile):
    n, d = q.shape
    tile = min(tile, seq)
    assert seq % tile == 0 and tile % CHUNK == 0
    nq = seq // tile
    kern = functools.partial(_attn_kernel, tile=tile)
    kv_spec = pl.BlockSpec((seq, HEAD_WIDTH), lambda b, h, i: (b, h))
    return pl.pallas_call(
        kern,
        out_shape=jax.ShapeDtypeStruct((n, d), BF16),
        grid=(batch, n_heads, nq),
        in_specs=[
            pl.BlockSpec((tile, HEAD_WIDTH), lambda b, h, i: (b * nq + i, h)),
            kv_spec,
            kv_spec,
            pl.BlockSpec((8, HEAD_DIM), lambda b, h, i: (0, 0)),
            pl.BlockSpec((None, 1, HEAD_DIM), lambda b, h, i: (h, 0, 0)),
            pl.BlockSpec((1, HEAD_WIDTH), lambda b, h, i: (0, 0)),
        ],
        out_specs=pl.BlockSpec((tile, HEAD_WIDTH), lambda b, h, i: (b * nq + i, h)),
        scratch_shapes=[
            pltpu.VMEM((2, tile, LANES), F32),
            pltpu.VMEM((2, tile, LANES), F32),
            pltpu.VMEM((2, tile, HEAD_WIDTH), F32),
            pltpu.VMEM((tile, tile), F32),
        ],
        compiler_params=pltpu.CompilerParams(
            dimension_semantics=("parallel", "parallel", "arbitrary"),
            vmem_limit_bytes=_vmem_limit(
                2 * _nbytes((seq, HEAD_WIDTH), BF16), 2 * _nbytes((tile, HEAD_WIDTH), BF16),
                scratch=4 * _nbytes((tile, LANES), F32) + 2 * _nbytes((tile, HEAD_WIDTH), F32)
                + 9 * _nbytes((tile, tile), F32))),
        name="diff_attention_prompt",
    )(q, k, v, lam_rows, slopes, g_subln.reshape(1, HEAD_WIDTH))


MXU_TILE = 256


def _attn_decode_kernel(q_ref, kp_ref, vp_ref, kn_ref, vn_ref, lam_ref, gs_ref, o_ref,
                        m_ref, l_ref, acc_ref, perm_ref, kh_ref, vh_ref,
                        *, tk, past, seq, n_heads):
    ki = pl.program_id(1)
    keys_per_group = MXU_TILE // n_heads

    @pl.when(ki == 0)
    def _():
        _init_softmax_state(m_ref, l_ref, acc_ref)
        r = lax.broadcasted_iota(jnp.int32, (MXU_TILE, MXU_TILE), 0)
        c = lax.broadcasted_iota(jnp.int32, (MXU_TILE, MXU_TILE), 1)
        src_row = (r % keys_per_group) * n_heads + r // keys_per_group
        perm_ref[...] = jnp.where(c == src_row, 1.0, 0.0).astype(perm_ref.dtype)

    def regroup_by_head(src_ref, dst_ref):
        perm = perm_ref[...]
        for g in range(tk // keys_per_group):
            rows = src_ref[g * MXU_TILE:(g + 1) * MXU_TILE, :].astype(BF16)
            by_head = jnp.dot(perm, rows, preferred_element_type=F32).astype(BF16)
            for h in range(n_heads):
                dst_ref[h, g * keys_per_group:(g + 1) * keys_per_group, :] = (
                    by_head[h * keys_per_group:(h + 1) * keys_per_group, :])

    def process(head_kv, kpos0, n_keys):
        qpos = lax.broadcasted_iota(jnp.int32, (seq, n_keys), 0) + past
        kpos = lax.broadcasted_iota(jnp.int32, (seq, n_keys), 1) + kpos0
        dist = jnp.abs(qpos - kpos).astype(F32)
        allowed = (kpos // CHUNK) <= (qpos // CHUNK)
        for h in range(n_heads):
            slope = LOG2E * 2.0 ** (-8.0 * (h + 1) / n_heads)
            bias = -slope * dist
            k, v = head_kv(h)
            for c in range(2):
                q = q_ref[:, h * HEAD_WIDTH + c * HEAD_DIM:h * HEAD_WIDTH + (c + 1) * HEAD_DIM]
                s = _qk_scores(q, k[:, c * HEAD_DIM:(c + 1) * HEAD_DIM]) + bias
                s = jnp.where(allowed, s, NEG_INF)
                _softmax_block_update(s, v, m_ref, l_ref, acc_ref, 2 * h + c)

    def new_head(h):
        cols = slice(h * HEAD_WIDTH, (h + 1) * HEAD_WIDTH)
        return kn_ref[:, cols], vn_ref[:, cols]

    regroup_by_head(kp_ref, kh_ref)
    regroup_by_head(vp_ref, vh_ref)
    process(lambda h: (kh_ref[h], vh_ref[h]), ki * tk, tk)

    @pl.when(ki == pl.num_programs(1) - 1)
    def _():
        process(new_head, past, seq)
        lam, lam_init = _lambda_value(lam_ref)
        for h in range(n_heads):
            o = _diff_head_output(acc_ref, l_ref, 2 * h, 2 * h + 1, lam, lam_init, gs_ref[...])
            o_ref[:, h * HEAD_WIDTH:(h + 1) * HEAD_WIDTH] = o.astype(o_ref.dtype)


def _attention_decode(q, k_new, v_new, k_past, v_past, layer, lam_rows, g_subln, *, batch, seq,
                      n_heads, tk):
    n, d = q.shape
    past = k_past.shape[1] // n_heads
    tk = min(tk, past)
    assert MXU_TILE % n_heads == 0 and tk % (MXU_TILE // n_heads) == 0
    kern = functools.partial(_attn_decode_kernel, tk=tk, past=past, seq=seq, n_heads=n_heads)
    past_spec = pl.BlockSpec((None, tk * n_heads, HEAD_WIDTH),
                             lambda b, j: (layer * batch + b, j, 0))
    new_spec = pl.BlockSpec((seq, d), lambda b, j: (b, 0))
    return pl.pallas_call(
        kern,
        out_shape=jax.ShapeDtypeStruct((n, d), BF16),
        grid=(batch, past // tk),
        in_specs=[
            new_spec, past_spec, past_spec, new_spec, new_spec,
            pl.BlockSpec((8, HEAD_DIM), lambda b, j: (0, 0)),
            pl.BlockSpec((1, HEAD_WIDTH), lambda b, j: (0, 0)),
        ],
        out_specs=new_spec,
        scratch_shapes=[
            pltpu.VMEM((2 * n_heads, seq, LANES), F32),
            pltpu.VMEM((2 * n_heads, seq, LANES), F32),
            pltpu.VMEM((2 * n_heads, seq, HEAD_WIDTH), F32),
            pltpu.VMEM((MXU_TILE, MXU_TILE), BF16),
            pltpu.VMEM((n_heads, tk, HEAD_WIDTH), BF16),
            pltpu.VMEM((n_heads, tk, HEAD_WIDTH), BF16),
        ],
        compiler_params=pltpu.CompilerParams(
            dimension_semantics=("parallel", "arbitrary"),
            vmem_limit_bytes=_vmem_limit(
                2 * _nbytes((tk, d), F32), 4 * _nbytes((seq, d), F32),
                scratch=4 * n_heads * _nbytes((seq, LANES), F32)
                + 2 * n_heads * _nbytes((seq, HEAD_WIDTH), F32) + 2 * _nbytes((tk, d), F32))),
        name="diff_attention_decode",
    )(q, k_past, v_past, k_new, v_new, lam_rows, g_subln.reshape(1, HEAD_WIDTH))


def _merge_kernel(z_ref, o_ref, gc_ref, ga_ref, wc_ref, wa_ref, out_ref):
    y_conv = jnp.dot(z_ref[...], wc_ref[...], preferred_element_type=F32)
    y_attn = jnp.dot(o_ref[...], wa_ref[...], preferred_element_type=F32)
    merged = (jax.nn.sigmoid(gc_ref[...].astype(F32)) * y_conv
              + jax.nn.sigmoid(ga_ref[...].astype(F32)) * y_attn)
    out_ref[...] = merged.astype(out_ref.dtype)


def _merge_branches(z, o, gates, w_conv_out, w_attn_out, layer, *, tm, tn=512):
    n, d = z.shape
    tm = min(tm, n)
    gate_blocks = d // tn
    return pl.pallas_call(
        _merge_kernel,
        out_shape=jax.ShapeDtypeStruct((n, d), BF16),
        grid=(n // tm, d // tn),
        in_specs=[
            pl.BlockSpec((tm, d), lambda i, j: (i, 0)),
            pl.BlockSpec((tm, d), lambda i, j: (i, 0)),
            pl.BlockSpec((tm, tn), lambda i, j: (i, j)),
            pl.BlockSpec((tm, tn), lambda i, j: (i, gate_blocks + j)),
            pl.BlockSpec((None, d, tn), lambda i, j: (layer, 0, j)),
            pl.BlockSpec((None, d, tn), lambda i, j: (layer, 0, j)),
        ],
        out_specs=pl.BlockSpec((tm, tn), lambda i, j: (i, j)),
        compiler_params=pltpu.CompilerParams(
            dimension_semantics=("parallel", "arbitrary"),
            vmem_limit_bytes=_vmem_limit(
                2 * _nbytes((tm, d), BF16), 2 * _nbytes((d, tn), BF16),
                3 * _nbytes((tm, tn), BF16), scratch=4 * _nbytes((tm, tn), F32))),
        name="merge_branches",
    )(z, o, gates, gates, w_conv_out, w_attn_out)


def _proj_residual_kernel(a_ref, w_ref, x_ref, gate_ref, o_ref):
    y = jnp.dot(a_ref[...], w_ref[...], preferred_element_type=F32)
    o_ref[...] = x_ref[...] + gate_ref[...] * y


def _proj_residual(a, w, layer, x, mod, *, which_gate, per_row, tokens_per_batch, tm, tn=1024):
    n, d = x.shape
    tm = min(tm, n)
    tiles_per_batch = max(tokens_per_batch // tm, 1)
    return pl.pallas_call(
        _proj_residual_kernel,
        out_shape=jax.ShapeDtypeStruct((n, d), F32),
        grid=(n // tm, d // tn),
        in_specs=[
            pl.BlockSpec((tm, a.shape[1]), lambda i, j: (i, 0)),
            pl.BlockSpec((None, a.shape[1], tn), lambda i, j: (layer, 0, j)),
            pl.BlockSpec((tm, tn), lambda i, j: (i, j)),
            _mod_spec(per_row, which_gate, tm, tn, tiles_per_batch, True),
        ],
        out_specs=pl.BlockSpec((tm, tn), lambda i, j: (i, j)),
        compiler_params=pltpu.CompilerParams(
            dimension_semantics=("parallel", "arbitrary"),
            vmem_limit_bytes=_vmem_limit(
                _nbytes((tm, a.shape[1]), BF16), _nbytes((a.shape[1], tn), BF16),
                3 * _nbytes((tm, tn), F32), scratch=_nbytes((tm, tn), F32))),
        name="proj_residual",
    )(a, w, x, mod)


def _mlp_kernel(x_ref, g_ref, sc_ref, sh_ref, gate_ref, wup_ref, wdn_ref, g_out_ref, o_ref,
                h_ref, acc_ref, *, norm_output):
    kf = pl.program_id(1)

    @pl.when(kf == 0)
    def _():
        h = _normmod(x_ref[...], g_ref[...], sc_ref[...], sh_ref[...])
        h_ref[...] = h.astype(h_ref.dtype)
        acc_ref[...] = jnp.zeros(acc_ref.shape, F32)

    up = jnp.dot(h_ref[...], wup_ref[...], preferred_element_type=F32)
    act = jnp.square(jnp.maximum(up, 0.0)).astype(BF16)
    acc_ref[...] += jnp.dot(act, wdn_ref[...], preferred_element_type=F32)

    @pl.when(kf == pl.num_programs(1) - 1)
    def _():
        y = x_ref[...] + gate_ref[...] * acc_ref[...]
        if norm_output:
            ms = jnp.mean(y * y, axis=-1, keepdims=True)
            y = (y * lax.rsqrt(ms + EPS)) * g_out_ref[...]
        o_ref[...] = y


def _mlp(x, g, mod, w_up, w_down, layer, g_out, *, norm_output, per_row, tokens_per_batch, tm,
         tf=1024):
    n, d = x.shape
    d_ff = w_up.shape[2]
    tm = min(tm, n)
    tiles_per_batch = max(tokens_per_batch // tm, 1)
    return pl.pallas_call(
        functools.partial(_mlp_kernel, norm_output=norm_output),
        out_shape=jax.ShapeDtypeStruct((n, d), F32),
        grid=(n // tm, d_ff // tf),
        in_specs=[
            pl.BlockSpec((tm, d), lambda i, j: (i, 0)),
            pl.BlockSpec((1, d), lambda i, j: (0, 0)),
            _mod_spec(per_row, 4, tm, d, tiles_per_batch, False),
            _mod_spec(per_row, 3, tm, d, tiles_per_batch, False),
            _mod_spec(per_row, 5, tm, d, tiles_per_batch, False),
            pl.BlockSpec((None, d, tf), lambda i, j: (layer, 0, j)),
            pl.BlockSpec((None, tf, d), lambda i, j: (layer, j, 0)),
            pl.BlockSpec((1, d), lambda i, j: (0, 0)),
        ],
        out_specs=pl.BlockSpec((tm, d), lambda i, j: (i, 0)),
        scratch_shapes=[pltpu.VMEM((tm, d), BF16), pltpu.VMEM((tm, d), F32)],
        compiler_params=pltpu.CompilerParams(
            dimension_semantics=("parallel", "arbitrary"),
            vmem_limit_bytes=_vmem_limit(
                2 * _nbytes((tm, d), F32), 2 * _nbytes((d, tf), BF16),
                3 * _nbytes((tm if per_row else 8, d), F32),
                scratch=_nbytes((tm, d), BF16) + _nbytes((tm, d), F32)
                + _nbytes((tm, tf), F32))),
        name="mlp",
    )(x, g.reshape(1, d), mod, mod, mod, w_up, w_down, g_out.reshape(1, d))


def _layer(x, mod, conv_prev, kv_past, kv_all, lam_rows, slopes, w, *, layer, n_layers, batch, seq,
           per_row, n_heads, tm):
    g_mix, w_in, conv_w, g_subln, w_conv_out, w_attn_out, w_o, g_mlp, w_up, w_down, g_final = w
    n, d = x.shape
    h = _norm_modulate(x, g_mix, mod, per_row=per_row, tokens_per_batch=seq, tm=512)
    if seq % tm == 0:
        z, new_conv = _inproj_conv(h, w_in, layer, conv_prev, conv_w, tokens_per_batch=seq,
                                   tm=tm)
    else:
        xgb = _project(h, w_in, layer, col_start=0, n_cols=3 * d, tm=tm, tn=1024,
                       name="in_proj_conv_inputs")
        z, new_conv = _gated_conv(xgb, conv_prev, conv_w, batch=batch, seq=seq, tt=512)
    q = _project(h, w_in, layer, col_start=3 * d, n_cols=d, scale=LOG2E * HEAD_DIM ** -0.5,
                 tm=tm, tn=d, name="in_proj_q")
    k_all, v_all = (None, None) if kv_all is None else kv_all
    k_bf, k_all = _project_kv(h, w_in, k_all, col_start=4 * d, layer=layer, n_layers=n_layers,
                              n_heads=n_heads, tm=512)
    v_bf, v_all = _project_kv(h, w_in, v_all, col_start=5 * d, layer=layer, n_layers=n_layers,
                              n_heads=n_heads, tm=512)
    gates = _project(h, w_in, layer, col_start=6 * d, n_cols=2 * d, tm=tm, tn=d,
                     name="in_proj_gates")

    if kv_past is None:
        o = _attention_prompt(q, k_bf, v_bf, lam_rows, slopes, g_subln, batch=batch, seq=seq,
                              n_heads=n_heads, tile=512)
    else:
        o = _attention_decode(q, k_bf, v_bf, *kv_past, lam_rows, g_subln,
                              batch=batch, seq=seq, n_heads=n_heads, tk=512)

    merged = _merge_branches(z, o, gates, w_conv_out, w_attn_out, layer, tm=tm)
    x = _proj_residual(merged, w_o, layer, x, mod, which_gate=2, per_row=per_row,
                       tokens_per_batch=seq, tm=tm)
    x = _mlp(x, g_mlp, mod, w_up, w_down, layer, g_final, norm_output=layer == n_layers - 1,
             per_row=per_row, tokens_per_batch=seq, tm=512)
    return x, (k_all, v_all), new_conv


def kernel(x_prompt, x_sample, c_prompt, c_sample, cache_k, cache_v, state_conv, w_ada, b_ada, g_mix, w_in, conv_w, lambda_q1, lambda_k1, lambda_q2, lambda_k2, g_subln, w_conv_out, w_attn_out, w_o, g_mlp, w_up, w_down, g_final):
    bp, tp, d = x_prompt.shape
    bs, ts, _ = x_sample.shape
    n_layers = w_in.shape[0]
    n_heads = cache_k.shape[3]

    mod_all = _modulation(jnp.concatenate([c_prompt, c_sample], axis=0), w_ada, b_ada)
    slopes = jnp.broadcast_to(
        LOG2E * jnp.exp2(-8.0 * (jnp.arange(n_heads, dtype=F32) + 1.0) / n_heads)[:, None, None],
        (n_heads, 1, HEAD_DIM))

    xp = x_prompt.reshape(bp * tp, d)
    xs = x_sample.reshape(bs * ts, d)
    conv_zero = jnp.zeros((bp, CONV_HALO, d), F32)
    past_len = cache_k.shape[2]
    past_k = cache_k.reshape(n_layers * bs, past_len * n_heads, HEAD_WIDTH)
    past_v = cache_v.reshape(n_layers * bs, past_len * n_heads, HEAD_WIDTH)
    w_in_bf, w_conv_out_bf, w_attn_out_bf, w_o_bf, w_up_bf, w_down_bf = (
        a.astype(BF16) for a in (w_in, w_conv_out, w_attn_out, w_o, w_up, w_down))
    conv_p, conv_s = [], []
    kv_p = kv_s = None
    for l in range(n_layers):
        lam_init = 0.8 - 0.6 * math.exp(-0.3 * l)
        lam_rows = jnp.concatenate(
            [lambda_q1[l][None], lambda_k1[l][None], lambda_q2[l][None], lambda_k2[l][None],
             jnp.full((4, HEAD_DIM), lam_init, F32)], axis=0).astype(F32)
        w = (g_mix[l], w_in_bf, conv_w[l], g_subln[l], w_conv_out_bf, w_attn_out_bf, w_o_bf,
             g_mlp[l], w_up_bf, w_down_bf, g_final)
        mod_p = mod_all[l, :bp].reshape(bp, 6, 1, d)
        mod_s = jnp.repeat(mod_all[l, bp:].reshape(bs, 6, d), ts, axis=0)
        mod_s = jnp.transpose(mod_s, (1, 0, 2))[None]
        xp, kv_p, cp = _layer(xp, mod_p, conv_zero, None, kv_p, lam_rows, slopes, w, layer=l,
                              n_layers=n_layers, batch=bp, seq=tp, per_row=False,
                              n_heads=n_heads, tm=1024)
        xs, kv_s, cs = _layer(xs, mod_s, state_conv[l], (past_k, past_v, l), kv_s, lam_rows,
                              slopes, w, layer=l, n_layers=n_layers, batch=bs, seq=ts,
                              per_row=True, n_heads=n_heads, tm=512)
        conv_p.append(cp)
        conv_s.append(cs)

    y_prompt = xp.reshape(bp, tp, d)
    y_sample = xs.reshape(bs, ts, d)
    kv_shape_p = (n_layers, bp, tp, n_heads, HEAD_WIDTH)
    kv_shape_s = (n_layers, bs, ts, n_heads, HEAD_WIDTH)
    return (y_prompt, y_sample,
            kv_p[0].reshape(kv_shape_p), kv_p[1].reshape(kv_shape_p), jnp.stack(conv_p),
            kv_s[0].reshape(kv_shape_s), kv_s[1].reshape(kv_shape_s), jnp.stack(conv_s))
```

```python
import functools
import math

import jax
import jax.numpy as jnp
from jax import lax
from jax.experimental import pallas as pl
from jax.experimental.pallas import tpu as pltpu

CHUNK = 64
CONV_WIDTH = 3
CONV_HALO = CONV_WIDTH - 1
EPS = 1e-6
NEG_INF = -1e30
HEAD_DIM = 128
HEAD_WIDTH = 2 * HEAD_DIM
LOG2E = math.log2(math.e)
LANES = 128

V7X_VMEM_BYTES = 64 * 1024 * 1024
VMEM_HEADROOM_BYTES = 8 * 1024 * 1024

BF16 = jnp.bfloat16
F32 = jnp.float32


def _vmem_limit(*block_bytes, scratch=0):
    need = 2 * sum(block_bytes) + scratch + VMEM_HEADROOM_BYTES
    return int(min(need, V7X_VMEM_BYTES - 4 * 1024 * 1024))


def _nbytes(shape, dtype):
    return math.prod(shape) * jnp.dtype(dtype).itemsize


def _mod_spec(per_row, which, tm, tn, tiles_per_batch, col_from_j):
    if per_row:
        return pl.BlockSpec((None, None, tm, tn),
                            lambda i, j: (0, which, i, j if col_from_j else 0))
    return pl.BlockSpec((None, None, 1, tn),
                        lambda i, j: (i // tiles_per_batch, which, 0, j if col_from_j else 0))


def _normmod(x, g, scale, shift):
    ms = jnp.mean(x * x, axis=-1, keepdims=True)
    return (x * lax.rsqrt(ms + EPS)) * g * (1.0 + scale) + shift


def _mod_kernel(c_ref, w_ref, b_ref, o_ref):
    c = c_ref[...]
    a = (c * jax.nn.sigmoid(c)).astype(BF16)
    y = jnp.dot(a, w_ref[...].astype(BF16), preferred_element_type=F32)
    o_ref[...] = y + b_ref[...]


def _modulation(c_all, w_ada, b_ada):
    n_layers, d, d6 = w_ada.shape
    rows = c_all.shape[0]
    tn = 1024
    return pl.pallas_call(
        _mod_kernel,
        out_shape=jax.ShapeDtypeStruct((n_layers, rows, d6), F32),
        grid=(n_layers, d6 // tn),
        in_specs=[
            pl.BlockSpec((rows, d), lambda l, j: (0, 0)),
            pl.BlockSpec((None, d, tn), lambda l, j: (l, 0, j)),
            pl.BlockSpec((None, 1, tn), lambda l, j: (l, 0, j)),
        ],
        out_specs=pl.BlockSpec((None, rows, tn), lambda l, j: (l, 0, j)),
        compiler_params=pltpu.CompilerParams(
            dimension_semantics=("arbitrary", "arbitrary"),
            vmem_limit_bytes=_vmem_limit(_nbytes((d, tn), F32), _nbytes((rows, d), F32),
                                         scratch=_nbytes((d, tn), BF16))),
        name="modulation",
    )(c_all, w_ada, b_ada.reshape(n_layers, 1, d6))


def _normmod_kernel(x_ref, g_ref, sc_ref, sh_ref, h_ref):
    h = _normmod(x_ref[...], g_ref[...], sc_ref[...], sh_ref[...])
    h_ref[...] = h.astype(h_ref.dtype)


def _norm_modulate(x, g, mod, *, per_row, tokens_per_batch, tm):
    n, d = x.shape
    tm = min(tm, n)
    tiles_per_batch = max(tokens_per_batch // tm, 1)
    return pl.pallas_call(
        _normmod_kernel,
        out_shape=jax.ShapeDtypeStruct((n, d), BF16),
        grid=(n // tm, 1),
        in_specs=[
            pl.BlockSpec((tm, d), lambda i, j: (i, 0)),
            pl.BlockSpec((1, d), lambda i, j: (0, 0)),
            _mod_spec(per_row, 1, tm, d, tiles_per_batch, False),
            _mod_spec(per_row, 0, tm, d, tiles_per_batch, False),
        ],
        out_specs=pl.BlockSpec((tm, d), lambda i, j: (i, 0)),
        compiler_params=pltpu.CompilerParams(
            dimension_semantics=("parallel", "arbitrary"),
            vmem_limit_bytes=_vmem_limit(
                _nbytes((tm, d), F32), _nbytes((tm, d), BF16),
                2 * _nbytes((tm if per_row else 8, d), F32), scratch=2 * _nbytes((tm, d), F32))),
        name="norm_modulate",
    )(x, g.reshape(1, d), mod, mod)


def _project_kernel(h_ref, w_ref, o_ref, *, scale):
    y = jnp.dot(h_ref[...], w_ref[...], preferred_element_type=F32)
    if scale != 1.0:
        y = y * scale
    o_ref[...] = y.astype(o_ref.dtype)


def _project(h, w, layer, *, col_start, n_cols, scale=1.0, tm, tn, name):
    n, d = h.shape
    tm = min(tm, n)
    col_blk0 = col_start // tn
    kern = functools.partial(_project_kernel, scale=scale)
    return pl.pallas_call(
        kern,
        out_shape=jax.ShapeDtypeStruct((n, n_cols), BF16),
        grid=(n // tm, n_cols // tn),
        in_specs=[
            pl.BlockSpec((tm, d), lambda i, j: (i, 0)),
            pl.BlockSpec((None, d, tn), lambda i, j: (layer, 0, col_blk0 + j)),
        ],
        out_specs=pl.BlockSpec((tm, tn), lambda i, j: (i, j)),
        compiler_params=pltpu.CompilerParams(
            dimension_semantics=("parallel", "arbitrary"),
            vmem_limit_bytes=_vmem_limit(
                _nbytes((tm, d), BF16), _nbytes((d, tn), BF16), _nbytes((tm, tn), BF16),
                scratch=2 * _nbytes((tm, tn), F32))),
        name=name,
    )(h, w)


def _project_kv_kernel(*refs, n_heads, n_carried):
    h_ref, w_ref = refs[:2]
    bf_ref, head_major_ref = refs[2 + n_carried:]
    y = jnp.dot(h_ref[...], w_ref[...], preferred_element_type=F32)
    bf_ref[...] = y.astype(bf_ref.dtype)
    for hd in range(n_heads):
        head_major_ref[:, hd, :] = y[:, hd * HEAD_WIDTH:(hd + 1) * HEAD_WIDTH]


def _project_kv(h, w, carried, *, col_start, layer, n_layers, n_heads, tm):
    n, d = h.shape
    tm = min(tm, n)
    carried = () if carried is None else (carried,)
    kern = functools.partial(_project_kv_kernel, n_heads=n_heads, n_carried=len(carried))
    return pl.pallas_call(
        kern,
        out_shape=(jax.ShapeDtypeStruct((n, d), BF16),
                   jax.ShapeDtypeStruct((n_layers, n, n_heads, HEAD_WIDTH), F32)),
        grid=(n // tm,),
        in_specs=[
            pl.BlockSpec((tm, d), lambda i: (i, 0)),
            pl.BlockSpec((None, d, d), lambda i: (layer, 0, col_start // d)),
        ] + [pl.BlockSpec(memory_space=pl.ANY)] * len(carried),
        out_specs=(pl.BlockSpec((tm, d), lambda i: (i, 0)),
                   pl.BlockSpec((None, tm, n_heads, HEAD_WIDTH), lambda i: (layer, i, 0, 0))),
        input_output_aliases={2: 1} if carried else {},
        compiler_params=pltpu.CompilerParams(
            dimension_semantics=("arbitrary",),
            vmem_limit_bytes=_vmem_limit(
                _nbytes((tm, d), BF16), _nbytes((d, d), BF16), _nbytes((tm, d), BF16),
                _nbytes((tm, d), F32), scratch=_nbytes((tm, d), F32))),
        name="in_proj_kv",
    )(h, w, *carried)


_CONV_PAD = 8


def _causal_conv(u, gb, w, upad_ref, rows):
    upad_ref[_CONV_PAD:_CONV_PAD + rows, :] = u
    yc = w[0:1, :] * upad_ref[_CONV_PAD - 2:_CONV_PAD - 2 + rows, :]
    yc = yc + w[1:2, :] * upad_ref[_CONV_PAD - 1:_CONV_PAD - 1 + rows, :]
    yc = yc + w[2:3, :] * u
    return gb * yc


def _conv_kernel(xin_ref, gc_ref, gb_ref, prev_ref, w_ref, z_ref, state_ref, upad_ref, *, tt):
    t = pl.program_id(1)

    @pl.when(t == 0)
    def _():
        upad_ref[_CONV_PAD - CONV_HALO:_CONV_PAD, :] = prev_ref[...]

    u = gc_ref[...].astype(F32) * xin_ref[...].astype(F32)
    z = _causal_conv(u, gb_ref[...].astype(F32), w_ref[...], upad_ref, tt)
    z_ref[...] = z.astype(z_ref.dtype)
    tail = upad_ref[_CONV_PAD + tt - CONV_HALO:_CONV_PAD + tt, :]
    state_ref[...] = tail
    upad_ref[_CONV_PAD - CONV_HALO:_CONV_PAD, :] = tail


def _gated_conv(xgb, conv_prev, conv_w, *, batch, seq, tt):
    n = xgb.shape[0]
    d = xgb.shape[1] // 3
    tt = min(tt, seq)
    nt = seq // tt
    kern = functools.partial(_conv_kernel, tt=tt)
    row = lambda b, t: b * nt + t
    return pl.pallas_call(
        kern,
        out_shape=(jax.ShapeDtypeStruct((n, d), BF16),
                   jax.ShapeDtypeStruct((batch, CONV_HALO, d), F32)),
        grid=(batch, nt),
        in_specs=[
            pl.BlockSpec((tt, d), lambda b, t: (row(b, t), 0)),
            pl.BlockSpec((tt, d), lambda b, t: (row(b, t), 1)),
            pl.BlockSpec((tt, d), lambda b, t: (row(b, t), 2)),
            pl.BlockSpec((None, CONV_HALO, d), lambda b, t: (b, 0, 0)),
            pl.BlockSpec((CONV_WIDTH, d), lambda b, t: (0, 0)),
        ],
        out_specs=(pl.BlockSpec((tt, d), lambda b, t: (row(b, t), 0)),
                   pl.BlockSpec((None, CONV_HALO, d), lambda b, t: (b, 0, 0))),
        scratch_shapes=[pltpu.VMEM((_CONV_PAD + tt, d), F32)],
        compiler_params=pltpu.CompilerParams(
            dimension_semantics=("parallel", "arbitrary"),
            vmem_limit_bytes=_vmem_limit(4 * _nbytes((tt, d), BF16),
                                         scratch=3 * _nbytes((_CONV_PAD + tt, d), F32))),
        name="gated_conv",
    )(xgb, xgb, xgb, conv_prev, conv_w)


def _inproj_conv_kernel(h_ref, wx_ref, wc_ref, wb_ref, prev_ref, cw_ref,
                        z_ref, state_ref, upad_ref, carry_ref, *, tm, tiles_per_batch):
    i = pl.program_id(0)
    j = pl.program_id(1)

    first = (i % tiles_per_batch) == 0

    @pl.when(first)
    def _():
        upad_ref[_CONV_PAD - CONV_HALO:_CONV_PAD, :] = prev_ref[...]

    @pl.when(jnp.logical_not(first))
    def _():
        upad_ref[_CONV_PAD - CONV_HALO:_CONV_PAD, :] = carry_ref[j]

    h = h_ref[...]
    xin = jnp.dot(h, wx_ref[...], preferred_element_type=F32)
    gc = jnp.dot(h, wc_ref[...], preferred_element_type=F32)
    gb = jnp.dot(h, wb_ref[...], preferred_element_type=F32)
    z = _causal_conv(gc * xin, gb, cw_ref[...], upad_ref, tm)
    z_ref[...] = z.astype(z_ref.dtype)
    tail = upad_ref[_CONV_PAD + tm - CONV_HALO:_CONV_PAD + tm, :]
    state_ref[...] = tail
    carry_ref[j] = tail


def _inproj_conv(h, w_in, layer, conv_prev, conv_w, *, tokens_per_batch, tm, tn=512):
    n, d = h.shape
    assert tokens_per_batch % tm == 0
    tiles_per_batch = tokens_per_batch // tm
    nb = d // tn
    batch = n // tokens_per_batch
    kern = functools.partial(_inproj_conv_kernel, tm=tm, tiles_per_batch=tiles_per_batch)
    w_spec = lambda seg: pl.BlockSpec((None, d, tn), lambda i, j: (layer, 0, seg * nb + j))
    z, tails = pl.pallas_call(
        kern,
        out_shape=(jax.ShapeDtypeStruct((n, d), BF16),
                   jax.ShapeDtypeStruct((n // tm, CONV_HALO, d), F32)),
        grid=(n // tm, nb),
        in_specs=[
            pl.BlockSpec((tm, d), lambda i, j: (i, 0)),
            w_spec(0), w_spec(1), w_spec(2),
            pl.BlockSpec((None, CONV_HALO, tn), lambda i, j: (i // tiles_per_batch, 0, j)),
            pl.BlockSpec((CONV_WIDTH, tn), lambda i, j: (0, j)),
        ],
        out_specs=(pl.BlockSpec((tm, tn), lambda i, j: (i, j)),
                   pl.BlockSpec((None, CONV_HALO, tn), lambda i, j: (i, 0, j))),
        scratch_shapes=[pltpu.VMEM((_CONV_PAD + tm, tn), F32),
                        pltpu.VMEM((nb, CONV_HALO, tn), F32)],
        compiler_params=pltpu.CompilerParams(
            dimension_semantics=("arbitrary", "arbitrary"),
            vmem_limit_bytes=_vmem_limit(
                _nbytes((tm, d), BF16), 3 * _nbytes((d, tn), BF16), _nbytes((tm, tn), BF16),
                scratch=6 * _nbytes((tm, tn), F32))),
        name="in_proj_conv",
    )(h, w_in, w_in, w_in, conv_prev, conv_w)
    return z, tails.reshape(batch, tiles_per_batch, CONV_HALO, d)[:, -1]


def _lambda_value(lam_ref):
    e1 = jnp.exp(jnp.sum(lam_ref[0:1, :] * lam_ref[1:2, :], axis=-1, keepdims=True))
    e2 = jnp.exp(jnp.sum(lam_ref[2:3, :] * lam_ref[3:4, :], axis=-1, keepdims=True))
    lam_init = lam_ref[4:5, 0:1]
    return e1 - e2 + lam_init, lam_init


def _widen(stat, width):
    if width < LANES:
        return stat[:, :width]
    return jnp.tile(stat, (1, width // LANES))


def _softmax_block_update(s, v, m_ref, l_ref, acc_ref, idx):
    m_prev = m_ref[idx]
    m_new = jnp.maximum(m_prev, jnp.max(s, axis=-1, keepdims=True))
    alpha = jnp.exp2(m_prev - m_new)
    p = jnp.exp2(s - _widen(m_new, s.shape[1]))
    l_ref[idx] = alpha * l_ref[idx] + jnp.sum(p, axis=-1, keepdims=True)
    pv = jnp.dot(p.astype(BF16), v, preferred_element_type=F32)
    acc_ref[idx] = _widen(alpha, HEAD_WIDTH) * acc_ref[idx] + pv
    m_ref[idx] = m_new


def _diff_head_output(acc_ref, l_ref, i1, i2, lam, lam_init, g_subln):
    o = (acc_ref[i1] / _widen(l_ref[i1], HEAD_WIDTH)
         - lam * (acc_ref[i2] / _widen(l_ref[i2], HEAD_WIDTH)))
    ms = jnp.mean(o * o, axis=-1, keepdims=True)
    return (o * lax.rsqrt(ms + EPS)) * g_subln * (1.0 - lam_init)


def _qk_scores(q, k):
    return lax.dot_general(q, k, (((1,), (1,)), ((), ())), preferred_element_type=F32)


def _init_softmax_state(m_ref, l_ref, acc_ref):
    m_ref[...] = jnp.full(m_ref.shape, NEG_INF, F32)
    l_ref[...] = jnp.zeros(l_ref.shape, F32)
    acc_ref[...] = jnp.zeros(acc_ref.shape, F32)


def _attn_kernel(q_ref, k_ref, v_ref, lam_ref, slope_ref, gs_ref, o_ref,
                 m_ref, l_ref, acc_ref, dbias_ref, *, tile):
    qi = pl.program_id(2)
    slope = slope_ref[0:1, 0:1]

    @pl.when(qi == 0)
    def _():
        t = lax.broadcasted_iota(jnp.int32, (tile, tile), 0)
        s = lax.broadcasted_iota(jnp.int32, (tile, tile), 1)
        bias = slope * (t - jnp.abs(t - s)).astype(F32)
        dbias_ref[...] = jnp.where((s // CHUNK) <= (t // CHUNK), bias, NEG_INF)

    _init_softmax_state(m_ref, l_ref, acc_ref)
    q = q_ref[...]

    def visit(ki, add_bias):
        k0 = pl.multiple_of(ki * tile, tile)
        k = k_ref[pl.ds(k0, tile), :]
        v = v_ref[pl.ds(k0, tile), :]
        for c in range(2):
            lanes = slice(c * HEAD_DIM, (c + 1) * HEAD_DIM)
            s = add_bias(_qk_scores(q[:, lanes], k[:, lanes]))
            _softmax_block_update(s, v, m_ref, l_ref, acc_ref, c)

    def earlier_tile(ki):
        rel = lax.broadcasted_iota(jnp.int32, (1, tile), 1) + (ki - qi) * tile
        col_bias = slope * rel.astype(F32)
        visit(ki, lambda s: s + col_bias)

    def earlier_quad(kq, carry):
        for u in range(4):
            earlier_tile(4 * kq + u)
        return carry

    lax.fori_loop(0, qi // 4, earlier_quad, 0)

    @pl.when(qi % 4 >= 2)
    def _():
        earlier_tile((qi // 4) * 4)
        earlier_tile((qi // 4) * 4 + 1)

    @pl.when(qi % 2 == 1)
    def _():
        earlier_tile(qi - 1)

    visit(qi, lambda s: s + dbias_ref[...])

    lam, lam_init = _lambda_value(lam_ref)
    o = _diff_head_output(acc_ref, l_ref, 0, 1, lam, lam_init, gs_ref[...])
    o_ref[...] = o.astype(o_ref.dtype)


def _attention_prompt(q, k, v, lam_rows, slopes, g_subln, *, batch, seq, n_heads, tile):
    n, d = q.shape
    tile = min(tile, seq)
    assert seq % tile == 0 and tile % CHUNK == 0
    nq = seq // tile
    kern = functools.partial(_attn_kernel, tile=tile)
    kv_spec = pl.BlockSpec((seq, HEAD_WIDTH), lambda b, h, i: (b, h))
    return pl.pallas_call(
        kern,
        out_shape=jax.ShapeDtypeStruct((n, d), BF16),
        grid=(batch, n_heads, nq),
        in_specs=[
            pl.BlockSpec((tile, HEAD_WIDTH), lambda b, h, i: (b * nq + i, h)),
            kv_spec,
            kv_spec,
            pl.BlockSpec((8, HEAD_DIM), lambda b, h, i: (0, 0)),
            pl.BlockSpec((None, 1, HEAD_DIM), lambda b, h, i: (h, 0, 0)),
            pl.BlockSpec((1, HEAD_WIDTH), lambda b, h, i: (0, 0)),
        ],
        out_specs=pl.BlockSpec((tile, HEAD_WIDTH), lambda b, h, i: (b * nq + i, h)),
        scratch_shapes=[
            pltpu.VMEM((2, tile, LANES), F32),
            pltpu.VMEM((2, tile, LANES), F32),
            pltpu.VMEM((2, tile, HEAD_WIDTH), F32),
            pltpu.VMEM((tile, tile), F32),
        ],
        compiler_params=pltpu.CompilerParams(
            dimension_semantics=("parallel", "parallel", "arbitrary"),
            vmem_limit_bytes=_vmem_limit(
                2 * _nbytes((seq, HEAD_WIDTH), BF16), 2 * _nbytes((tile, HEAD_WIDTH), BF16),
                scratch=4 * _nbytes((tile, LANES), F32) + 2 * _nbytes((tile, HEAD_WIDTH), F32)
                + 9 * _nbytes((tile, tile), F32))),
        name="diff_attention_prompt",
    )(q, k, v, lam_rows, slopes, g_subln.reshape(1, HEAD_WIDTH))


MXU_TILE = 256


def _attn_decode_kernel(q_ref, kp_ref, vp_ref, kn_ref, vn_ref, lam_ref, gs_ref, o_ref,
                        m_ref, l_ref, acc_ref, perm_ref, kh_ref, vh_ref,
                        *, tk, past, seq, n_heads):
    ki = pl.program_id(1)
    keys_per_group = MXU_TILE // n_heads

    @pl.when(ki == 0)
    def _():
        _init_softmax_state(m_ref, l_ref, acc_ref)
        r = lax.broadcasted_iota(jnp.int32, (MXU_TILE, MXU_TILE), 0)
        c = lax.broadcasted_iota(jnp.int32, (MXU_TILE, MXU_TILE), 1)
        src_row = (r % keys_per_group) * n_heads + r // keys_per_group
        perm_ref[...] = jnp.where(c == src_row, 1.0, 0.0).astype(perm_ref.dtype)

    def regroup_by_head(src_ref, dst_ref):
        perm = perm_ref[...]
        for g in range(tk // keys_per_group):
            rows = src_ref[g * MXU_TILE:(g + 1) * MXU_TILE, :].astype(BF16)
            by_head = jnp.dot(perm, rows, preferred_element_type=F32).astype(BF16)
            for h in range(n_heads):
                dst_ref[h, g * keys_per_group:(g + 1) * keys_per_group, :] = (
                    by_head[h * keys_per_group:(h + 1) * keys_per_group, :])

    def process(head_kv, kpos0, n_keys):
        qpos = lax.broadcasted_iota(jnp.int32, (seq, n_keys), 0) + past
        kpos = lax.broadcasted_iota(jnp.int32, (seq, n_keys), 1) + kpos0
        dist = jnp.abs(qpos - kpos).astype(F32)
        allowed = (kpos // CHUNK) <= (qpos // CHUNK)
        for h in range(n_heads):
            slope = LOG2E * 2.0 ** (-8.0 * (h + 1) / n_heads)
            bias = -slope * dist
            k, v = head_kv(h)
            for c in range(2):
                q = q_ref[:, h * HEAD_WIDTH + c * HEAD_DIM:h * HEAD_WIDTH + (c + 1) * HEAD_DIM]
                s = _qk_scores(q, k[:, c * HEAD_DIM:(c + 1) * HEAD_DIM]) + bias
                s = jnp.where(allowed, s, NEG_INF)
                _softmax_block_update(s, v, m_ref, l_ref, acc_ref, 2 * h + c)

    def new_head(h):
        cols = slice(h * HEAD_WIDTH, (h + 1) * HEAD_WIDTH)
        return kn_ref[:, cols], vn_ref[:, cols]

    regroup_by_head(kp_ref, kh_ref)
    regroup_by_head(vp_ref, vh_ref)
    process(lambda h: (kh_ref[h], vh_ref[h]), ki * tk, tk)

    @pl.when(ki == pl.num_programs(1) - 1)
    def _():
        process(new_head, past, seq)
        lam, lam_init = _lambda_value(lam_ref)
        for h in range(n_heads):
            o = _diff_head_output(acc_ref, l_ref, 2 * h, 2 * h + 1, lam, lam_init, gs_ref[...])
            o_ref[:, h * HEAD_WIDTH:(h + 1) * HEAD_WIDTH] = o.astype(o_ref.dtype)


def _attention_decode(q, k_new, v_new, k_past, v_past, layer, lam_rows, g_subln, *, batch, seq,
                      n_heads, tk):
    n, d = q.shape
    past = k_past.shape[1] // n_heads
    tk = min(tk, past)
    assert MXU_TILE % n_heads == 0 and tk % (MXU_TILE // n_heads) == 0
    kern = functools.partial(_attn_decode_kernel, tk=tk, past=past, seq=seq, n_heads=n_heads)
    past_spec = pl.BlockSpec((None, tk * n_heads, HEAD_WIDTH),
                             lambda b, j: (layer * batch + b, j, 0))
    new_spec = pl.BlockSpec((seq, d), lambda b, j: (b, 0))
    return pl.pallas_call(
        kern,
        out_shape=jax.ShapeDtypeStruct((n, d), BF16),
        grid=(batch, past // tk),
        in_specs=[
            new_spec, past_spec, past_spec, new_spec, new_spec,
            pl.BlockSpec((8, HEAD_DIM), lambda b, j: (0, 0)),
            pl.BlockSpec((1, HEAD_WIDTH), lambda b, j: (0, 0)),
        ],
        out_specs=new_spec,
        scratch_shapes=[
            pltpu.VMEM((2 * n_heads, seq, LANES), F32),
            pltpu.VMEM((2 * n_heads, seq, LANES), F32),
            pltpu.VMEM((2 * n_heads, seq, HEAD_WIDTH), F32),
            pltpu.VMEM((MXU_TILE, MXU_TILE), BF16),
            pltpu.VMEM((n_heads, tk, HEAD_WIDTH), BF16),
            pltpu.VMEM((n_heads, tk, HEAD_WIDTH), BF16),
        ],
        compiler_params=pltpu.CompilerParams(
            dimension_semantics=("parallel", "arbitrary"),
            vmem_limit_bytes=_vmem_limit(
                2 * _nbytes((tk, d), F32), 4 * _nbytes((seq, d), F32),
                scratch=4 * n_heads * _nbytes((seq, LANES), F32)
                + 2 * n_heads * _nbytes((seq, HEAD_WIDTH), F32) + 2 * _nbytes((tk, d), F32))),
        name="diff_attention_decode",
    )(q, k_past, v_past, k_new, v_new, lam_rows, g_subln.reshape(1, HEAD_WIDTH))


def _merge_kernel(z_ref, o_ref, gc_ref, ga_ref, wc_ref, wa_ref, out_ref):
    y_conv = jnp.dot(z_ref[...], wc_ref[...], preferred_element_type=F32)
    y_attn = jnp.dot(o_ref[...], wa_ref[...], preferred_element_type=F32)
    merged = (jax.nn.sigmoid(gc_ref[...].astype(F32)) * y_conv
              + jax.nn.sigmoid(ga_ref[...].astype(F32)) * y_attn)
    out_ref[...] = merged.astype(out_ref.dtype)


def _merge_branches(z, o, gates, w_conv_out, w_attn_out, layer, *, tm, tn=512):
    n, d = z.shape
    tm = min(tm, n)
    gate_blocks = d // tn
    return pl.pallas_call(
        _merge_kernel,
        out_shape=jax.ShapeDtypeStruct((n, d), BF16),
        grid=(n // tm, d // tn),
        in_specs=[
            pl.BlockSpec((tm, d), lambda i, j: (i, 0)),
            pl.BlockSpec((tm, d), lambda i, j: (i, 0)),
            pl.BlockSpec((tm, tn), lambda i, j: (i, j)),
            pl.BlockSpec((tm, tn), lambda i, j: (i, gate_blocks + j)),
            pl.BlockSpec((None, d, tn), lambda i, j: (layer, 0, j)),
            pl.BlockSpec((None, d, tn), lambda i, j: (layer, 0, j)),
        ],
        out_specs=pl.BlockSpec((tm, tn), lambda i, j: (i, j)),
        compiler_params=pltpu.CompilerParams(
            dimension_semantics=("parallel", "arbitrary"),
            vmem_limit_bytes=_vmem_limit(
                2 * _nbytes((tm, d), BF16), 2 * _nbytes((d, tn), BF16),
                3 * _nbytes((tm, tn), BF16), scratch=4 * _nbytes((tm, tn), F32))),
        name="merge_branches",
    )(z, o, gates, gates, w_conv_out, w_attn_out)


def _proj_residual_kernel(a_ref, w_ref, x_ref, gate_ref, o_ref):
    y = jnp.dot(a_ref[...], w_ref[...], preferred_element_type=F32)
    o_ref[...] = x_ref[...] + gate_ref[...] * y


def _proj_residual(a, w, layer, x, mod, *, which_gate, per_row, tokens_per_batch, tm, tn=1024):
    n, d = x.shape
    tm = min(tm, n)
    tiles_per_batch = max(tokens_per_batch // tm, 1)
    return pl.pallas_call(
        _proj_residual_kernel,
        out_shape=jax.ShapeDtypeStruct((n, d), F32),
        grid=(n // tm, d // tn),
        in_specs=[
            pl.BlockSpec((tm, a.shape[1]), lambda i, j: (i, 0)),
            pl.BlockSpec((None, a.shape[1], tn), lambda i, j: (layer, 0, j)),
            pl.BlockSpec((tm, tn), lambda i, j: (i, j)),
            _mod_spec(per_row, which_gate, tm, tn, tiles_per_batch, True),
        ],
        out_specs=pl.BlockSpec((tm, tn), lambda i, j: (i, j)),
        compiler_params=pltpu.CompilerParams(
            dimension_semantics=("parallel", "arbitrary"),
            vmem_limit_bytes=_vmem_limit(
                _nbytes((tm, a.shape[1]), BF16), _nbytes((a.shape[1], tn), BF16),
                3 * _nbytes((tm, tn), F32), scratch=_nbytes((tm, tn), F32))),
        name="proj_residual",
    )(a, w, x, mod)


def _mlp_kernel(*refs, norm_output, emit_next):
    x_ref, g_ref, sc_ref, sh_ref, gate_ref, wup_ref, wdn_ref, g_out_ref = refs[:8]
    if emit_next:
        g_next_ref, sc_next_ref, sh_next_ref, o_ref, h_next_ref, h_ref, acc_ref = refs[8:]
    else:
        o_ref, h_ref, acc_ref = refs[8:]
    kf = pl.program_id(1)

    @pl.when(kf == 0)
    def _():
        h = _normmod(x_ref[...], g_ref[...], sc_ref[...], sh_ref[...])
        h_ref[...] = h.astype(h_ref.dtype)
        acc_ref[...] = jnp.zeros(acc_ref.shape, F32)

    up = jnp.dot(h_ref[...], wup_ref[...], preferred_element_type=F32)
    act = jnp.square(jnp.maximum(up, 0.0)).astype(BF16)
    acc_ref[...] += jnp.dot(act, wdn_ref[...], preferred_element_type=F32)

    @pl.when(kf == pl.num_programs(1) - 1)
    def _():
        y = x_ref[...] + gate_ref[...] * acc_ref[...]
        if norm_output:
            ms = jnp.mean(y * y, axis=-1, keepdims=True)
            y = (y * lax.rsqrt(ms + EPS)) * g_out_ref[...]
        o_ref[...] = y
        if emit_next:
            h_next = _normmod(y, g_next_ref[...], sc_next_ref[...], sh_next_ref[...])
            h_next_ref[...] = h_next.astype(h_next_ref.dtype)


def _mlp(x, g, mod, w_up, w_down, layer, g_out, next_norm, *, norm_output, per_row,
         tokens_per_batch, tm, tf=1024):
    n, d = x.shape
    d_ff = w_up.shape[2]
    tm = min(tm, n)
    tiles_per_batch = max(tokens_per_batch // tm, 1)
    emit_next = next_norm is not None
    row_block = pl.BlockSpec((tm, d), lambda i, j: (i, 0))
    next_specs, next_args, out_shape, out_specs = [], [], jax.ShapeDtypeStruct((n, d), F32), row_block
    if emit_next:
        g_next, mod_next = next_norm
        next_specs = [pl.BlockSpec((1, d), lambda i, j: (0, 0)),
                      _mod_spec(per_row, 1, tm, d, tiles_per_batch, False),
                      _mod_spec(per_row, 0, tm, d, tiles_per_batch, False)]
        next_args = [g_next.reshape(1, d), mod_next, mod_next]
        out_shape = (out_shape, jax.ShapeDtypeStruct((n, d), BF16))
        out_specs = (row_block, row_block)
    return pl.pallas_call(
        functools.partial(_mlp_kernel, norm_output=norm_output, emit_next=emit_next),
        out_shape=out_shape,
        grid=(n // tm, d_ff // tf),
        in_specs=[
            pl.BlockSpec((tm, d), lambda i, j: (i, 0)),
            pl.BlockSpec((1, d), lambda i, j: (0, 0)),
            _mod_spec(per_row, 4, tm, d, tiles_per_batch, False),
            _mod_spec(per_row, 3, tm, d, tiles_per_batch, False),
            _mod_spec(per_row, 5, tm, d, tiles_per_batch, False),
            pl.BlockSpec((None, d, tf), lambda i, j: (layer, 0, j)),
            pl.BlockSpec((None, tf, d), lambda i, j: (layer, j, 0)),
            pl.BlockSpec((1, d), lambda i, j: (0, 0)),
        ] + next_specs,
        out_specs=out_specs,
        scratch_shapes=[pltpu.VMEM((tm, d), BF16), pltpu.VMEM((tm, d), F32)],
        compiler_params=pltpu.CompilerParams(
            dimension_semantics=("parallel", "arbitrary"),
            vmem_limit_bytes=_vmem_limit(
                2 * _nbytes((tm, d), F32), 2 * _nbytes((d, tf), BF16),
                3 * _nbytes((tm if per_row else 8, d), F32), _nbytes((tm, d), BF16),
                scratch=_nbytes((tm, d), BF16) + _nbytes((tm, d), F32)
                + _nbytes((tm, tf), F32))),
        name="mlp",
    )(x, g.reshape(1, d), mod, mod, mod, w_up, w_down, g_out.reshape(1, d), *next_args)


def _layer(x, h, mod, next_norm, conv_prev, kv_past, kv_all, lam_rows, slopes, w, *, layer,
           n_layers, batch, seq, per_row, n_heads, tm):
    g_mix, w_in, conv_w, g_subln, w_conv_out, w_attn_out, w_o, g_mlp, w_up, w_down, g_final = w
    n, d = x.shape
    if h is None:
        h = _norm_modulate(x, g_mix, mod, per_row=per_row, tokens_per_batch=seq, tm=512)
    if seq % tm == 0:
        z, new_conv = _inproj_conv(h, w_in, layer, conv_prev, conv_w, tokens_per_batch=seq,
                                   tm=tm)
    else:
        xgb = _project(h, w_in, layer, col_start=0, n_cols=3 * d, tm=tm, tn=1024,
                       name="in_proj_conv_inputs")
        z, new_conv = _gated_conv(xgb, conv_prev, conv_w, batch=batch, seq=seq, tt=512)
    q = _project(h, w_in, layer, col_start=3 * d, n_cols=d, scale=LOG2E * HEAD_DIM ** -0.5,
                 tm=tm, tn=d, name="in_proj_q")
    k_all, v_all = (None, None) if kv_all is None else kv_all
    k_bf, k_all = _project_kv(h, w_in, k_all, col_start=4 * d, layer=layer, n_layers=n_layers,
                              n_heads=n_heads, tm=512)
    v_bf, v_all = _project_kv(h, w_in, v_all, col_start=5 * d, layer=layer, n_layers=n_layers,
                              n_heads=n_heads, tm=512)
    gates = _project(h, w_in, layer, col_start=6 * d, n_cols=2 * d, tm=tm, tn=d,
                     name="in_proj_gates")

    if kv_past is None:
        o = _attention_prompt(q, k_bf, v_bf, lam_rows, slopes, g_subln, batch=batch, seq=seq,
                              n_heads=n_heads, tile=512)
    else:
        o = _attention_decode(q, k_bf, v_bf, *kv_past, lam_rows, g_subln,
                              batch=batch, seq=seq, n_heads=n_heads, tk=512)

    merged = _merge_branches(z, o, gates, w_conv_out, w_attn_out, layer, tm=tm, tn=1024)
    x = _proj_residual(merged, w_o, layer, x, mod, which_gate=2, per_row=per_row,
                       tokens_per_batch=seq, tm=tm)
    out = _mlp(x, g_mlp, mod, w_up, w_down, layer, g_final, next_norm,
               norm_output=layer == n_layers - 1, per_row=per_row, tokens_per_batch=seq, tm=512)
    x, h_next = out if next_norm is not None else (out, None)
    return x, h_next, (k_all, v_all), new_conv


def kernel(x_prompt, x_sample, c_prompt, c_sample, cache_k, cache_v, state_conv, w_ada, b_ada, g_mix, w_in, conv_w, lambda_q1, lambda_k1, lambda_q2, lambda_k2, g_subln, w_conv_out, w_attn_out, w_o, g_mlp, w_up, w_down, g_final):
    bp, tp, d = x_prompt.shape
    bs, ts, _ = x_sample.shape
    n_layers = w_in.shape[0]
    n_heads = cache_k.shape[3]

    mod_all = _modulation(jnp.concatenate([c_prompt, c_sample], axis=0), w_ada, b_ada)
    slopes = jnp.broadcast_to(
        LOG2E * jnp.exp2(-8.0 * (jnp.arange(n_heads, dtype=F32) + 1.0) / n_heads)[:, None, None],
        (n_heads, 1, HEAD_DIM))

    xp = x_prompt.reshape(bp * tp, d)
    xs = x_sample.reshape(bs * ts, d)
    conv_zero = jnp.zeros((bp, CONV_HALO, d), F32)
    past_len = cache_k.shape[2]
    past_k = cache_k.reshape(n_layers * bs, past_len * n_heads, HEAD_WIDTH)
    past_v = cache_v.reshape(n_layers * bs, past_len * n_heads, HEAD_WIDTH)
    w_in_bf, w_conv_out_bf, w_attn_out_bf, w_o_bf, w_up_bf, w_down_bf = (
        a.astype(BF16) for a in (w_in, w_conv_out, w_attn_out, w_o, w_up, w_down))
    conv_p, conv_s = [], []
    kv_p = kv_s = None
    mods_p, mods_s = [], []
    for l in range(n_layers):
        mods_p.append(mod_all[l, :bp].reshape(bp, 6, 1, d))
        mod_s = jnp.repeat(mod_all[l, bp:].reshape(bs, 6, d), ts, axis=0)
        mods_s.append(jnp.transpose(mod_s, (1, 0, 2))[None])
    hp = None
    for l in range(n_layers):
        lam_init = 0.8 - 0.6 * math.exp(-0.3 * l)
        lam_rows = jnp.concatenate(
            [lambda_q1[l][None], lambda_k1[l][None], lambda_q2[l][None], lambda_k2[l][None],
             jnp.full((4, HEAD_DIM), lam_init, F32)], axis=0).astype(F32)
        w = (g_mix[l], w_in_bf, conv_w[l], g_subln[l], w_conv_out_bf, w_attn_out_bf, w_o_bf,
             g_mlp[l], w_up_bf, w_down_bf, g_final)
        next_p = (g_mix[l + 1], mods_p[l + 1]) if l + 1 < n_layers else None
        xp, hp, kv_p, cp = _layer(xp, hp, mods_p[l], next_p, conv_zero, None, kv_p, lam_rows,
                                  slopes, w, layer=l, n_layers=n_layers, batch=bp, seq=tp,
                                  per_row=False, n_heads=n_heads, tm=1024)
        xs, _, kv_s, cs = _layer(xs, None, mods_s[l], None, state_conv[l], (past_k, past_v, l),
                                 kv_s, lam_rows, slopes, w, layer=l, n_layers=n_layers, batch=bs,
                                 seq=ts, per_row=True, n_heads=n_heads, tm=512)
        conv_p.append(cp)
        conv_s.append(cs)

    y_prompt = xp.reshape(bp, tp, d)
    y_sample = xs.reshape(bs, ts, d)
    kv_shape_p = (n_layers, bp, tp, n_heads, HEAD_WIDTH)
    kv_shape_s = (n_layers, bs, ts, n_heads, HEAD_WIDTH)
    return (y_prompt, y_sample,
            kv_p[0].reshape(kv_shape_p), kv_p[1].reshape(kv_shape_p), jnp.stack(conv_p),
            kv_s[0].reshape(kv_shape_s), kv_s[1].reshape(kv_shape_s), jnp.stack(conv_s))
```
